```python
import jax
import jax.numpy as jnp
from jax import lax
import numpy as np

D_MODEL = 1024
BATCH = 16
SEQ = 4096
DEPTH = 1

CHUNK = 64
N_META = 16
Q_BLOCK = 128
FOX_HEADS = 8
FOX_HEAD_DIM = 64
FOX_WIDTH = FOX_HEADS * FOX_HEAD_DIM
HG_HEADS = 4
HG_KEY_DIM = 128
HG_VAL_DIM = 128
HG_KWIDTH = HG_HEADS * HG_KEY_DIM
HG_VWIDTH = HG_HEADS * HG_VAL_DIM
N_BRANCH = 2
D_FF = 2816
CONV_WIDTH = 3
EPS = 1e-6
SPLIT_SIZES = (FOX_WIDTH, FOX_WIDTH, FOX_WIDTH, FOX_HEADS,
               HG_KWIDTH, HG_KWIDTH, HG_VWIDTH, HG_VWIDTH,
               D_MODEL, D_MODEL)
IN_COLS = sum(SPLIT_SIZES)
SPLIT_POINTS = tuple(int(v) for v in np.cumsum(SPLIT_SIZES)[:-1])

kernel_name = 'hybrid_hgrn2_fox_gated_block'


def _rms(x, gain):
    xf = x.astype(jnp.float32)
    y = xf * lax.rsqrt(jnp.mean(xf * xf, axis=-1, keepdims=True) + EPS)
    return (y * gain.astype(jnp.float32)).astype(x.dtype)


def _fox_attention(q, k, v, f_logit):
    b, nh, l, dh = q.shape
    n_blk = -(-l // Q_BLOCK)
    lp = n_blk * Q_BLOCK
    pad4 = ((0, 0), (0, 0), (0, lp - l), (0, 0))
    qp = jnp.pad(q, pad4)
    kp = jnp.pad(k, pad4)
    vp = jnp.pad(v, pad4)
    logf = jax.nn.log_sigmoid(f_logit.astype(jnp.float32))
    cum = jnp.cumsum(jnp.pad(logf, ((0, 0), (0, 0), (0, lp - l))), axis=-1)
    scale = dh ** -0.5
    outs = []
    for i in range(n_blk):
        start, stop = i * Q_BLOCK, (i + 1) * Q_BLOCK
        qb = qp[:, :, start:stop]
        s = jnp.einsum('bhqd,bhkd->bhqk', qb, kp[:, :, :stop],
                       preferred_element_type=jnp.float32) * scale
        s = s + (cum[:, :, start:stop, None] - cum[:, :, None, :stop])
        causal = jnp.arange(stop)[None, :] <= jnp.arange(start, stop)[:, None]
        s = jnp.where(causal, s, -jnp.inf)
        p = jax.nn.softmax(s, axis=-1)
        outs.append(jnp.einsum('bhqk,bhkd->bhqd', p.astype(vp.dtype), vp[:, :, :stop]))
    out = jnp.concatenate(outs, axis=2)
    return out[:, :, :l]


def _hgrn2(q, log_f, k, v):
    b, l, nh, dk = q.shape
    dv = v.shape[-1]
    lead = (-N_META) % CHUNK
    tail = (-(l + lead)) % CHUNK
    n_chunk = (l + lead + tail) // CHUNK
    pad = ((0, 0), (lead, tail), (0, 0), (0, 0))

    def to_chunks(a):
        a = jnp.pad(a.astype(jnp.float32), pad)
        a = a.reshape(b, n_chunk, CHUNK, nh, a.shape[-1])
        return jnp.transpose(a, (2, 0, 1, 3, 4))

    qc, fc, kc, vc = to_chunks(q), to_chunks(log_f), to_chunks(k), to_chunks(v)

    def step(s, inp):
        q_t, f_t, k_t, v_t = inp
        s = jnp.exp(f_t)[..., None] * s + k_t[..., None] * v_t[..., None, :]
        return s, jnp.einsum('bnhk,bnhkv->bnhv', q_t, s)

    s0 = jnp.zeros((b, n_chunk, nh, dk, dv), jnp.float32)
    s_local, o_local = lax.scan(step, s0, (qc, fc, kc, vc))
    g = jnp.cumsum(fc, axis=0)

    def carry_step(s, inp):
        s_loc, g_tot = inp
        return jnp.exp(g_tot)[..., None] * s + s_loc, s

    _, s_prev = lax.scan(carry_step, jnp.zeros((b, nh, dk, dv), jnp.float32),
                         (jnp.moveaxis(s_local, 1, 0), jnp.moveaxis(g[-1], 1, 0)))
    s_prev = jnp.moveaxis(s_prev, 0, 1)
    o = o_local + jnp.einsum('cbnhk,bnhkv->cbnhv', qc * jnp.exp(g), s_prev)
    o = jnp.transpose(o, (1, 2, 0, 3, 4)).reshape(b, n_chunk * CHUNK, nh, dv)
    return o[:, lead:lead + l]


def _causal_dwconv(u, w, bias):
    c = u.shape[-1]
    y = lax.conv_general_dilated(u, w[:, None, :].astype(u.dtype), window_strides=(1,),
                                 padding=[(CONV_WIDTH - 1, 0)],
                                 dimension_numbers=('NWC', 'WIO', 'NWC'),
                                 feature_group_count=c)
    return y + bias.astype(u.dtype)


def setup_inputs(seed: int = 0) -> dict:
    key = jax.random.key(seed)
    ks = jax.random.split(key, 18)
    f32 = jnp.float32

    def nrm(k, shape, scale):
        return jax.random.normal(k, shape, f32) * scale

    return {
        'x': nrm(ks[0], (BATCH, SEQ, D_MODEL), 1.0),
        'meta_tokens': nrm(ks[1], (N_META, D_MODEL), 1.0),
        'norm1_gain': 1.0 + nrm(ks[2], (DEPTH, D_MODEL), 0.02),
        'w_in': nrm(ks[3], (DEPTH, D_MODEL, IN_COLS), D_MODEL ** -0.5),
        'fox_b_f': 3.0 + nrm(ks[4], (DEPTH, FOX_HEADS), 0.5),
        'q_norm_gain': 1.0 + nrm(ks[5], (DEPTH, FOX_HEAD_DIM), 0.02),
        'k_norm_gain': 1.0 + nrm(ks[6], (DEPTH, FOX_HEAD_DIM), 0.02),
        'hg_lb_logits': nrm(ks[7], (DEPTH + 1, HG_KWIDTH), 0.1),
        'hg_out_gain': 1.0 + nrm(ks[8], (DEPTH, HG_VAL_DIM), 0.02),
        'w_branch_a': nrm(ks[9], (DEPTH, HG_VWIDTH, D_MODEL), HG_VWIDTH ** -0.5),
        'w_branch_b': nrm(ks[10], (DEPTH, FOX_WIDTH, D_MODEL), FOX_WIDTH ** -0.5),
        'w_out': nrm(ks[11], (DEPTH, D_MODEL, D_MODEL), D_MODEL ** -0.5),
        'norm2_gain': 1.0 + nrm(ks[12], (DEPTH, D_MODEL), 0.02),
        'w_up': nrm(ks[13], (DEPTH, D_MODEL, 2 * D_FF), D_MODEL ** -0.5),
        'conv_w': nrm(ks[14], (DEPTH, CONV_WIDTH, 2 * D_FF), CONV_WIDTH ** -0.5),
        'conv_b': nrm(ks[15], (DEPTH, 2 * D_FF), 0.02),
        'w_down': nrm(ks[16], (DEPTH, D_FF, D_MODEL), D_FF ** -0.5),
    }


def reference(x, meta_tokens, norm1_gain, w_in, fox_b_f, q_norm_gain, k_norm_gain,
              hg_lb_logits, hg_out_gain, w_branch_a, w_branch_b, w_out, norm2_gain,
              w_up, conv_w, conv_b, w_down):
    b = x.shape[0]
    meta = jnp.broadcast_to(meta_tokens[None].astype(x.dtype), (b, N_META, D_MODEL))
    h = jnp.concatenate([meta, x], axis=1)
    l = h.shape[1]
    lower_bounds = jnp.cumsum(jax.nn.softmax(hg_lb_logits.astype(jnp.float32), axis=0), axis=0)

    for layer in range(DEPTH):
        xn = _rms(h, norm1_gain[layer])
        proj = xn @ w_in[layer]
        (fq, fk, fv, ff, hq, hf, hi, hg, gate_a, gate_b) = jnp.split(proj, SPLIT_POINTS, axis=-1)

        fq = _rms(fq.reshape(b, l, FOX_HEADS, FOX_HEAD_DIM), q_norm_gain[layer])
        fk = _rms(fk.reshape(b, l, FOX_HEADS, FOX_HEAD_DIM), k_norm_gain[layer])
        fv = fv.reshape(b, l, FOX_HEADS, FOX_HEAD_DIM)
        f_logit = jnp.transpose(ff + fox_b_f[layer].astype(ff.dtype), (0, 2, 1))
        o_fox = _fox_attention(jnp.transpose(fq, (0, 2, 1, 3)), jnp.transpose(fk, (0, 2, 1, 3)),
                               jnp.transpose(fv, (0, 2, 1, 3)), f_logit)
        o_fox = jnp.transpose(o_fox, (0, 2, 1, 3)).reshape(b, l, FOX_WIDTH)

        lb = lower_bounds[layer].reshape(HG_HEADS, HG_KEY_DIM)
        hf32 = hf.astype(jnp.float32).reshape(b, l, HG_HEADS, HG_KEY_DIM)
        log_f = jnp.log(lb + (1.0 - lb) * jax.nn.sigmoid(hf32))
        k_in = (1.0 - lb) * jax.nn.sigmoid(-hf32)
        o_hg = _hgrn2(hq.reshape(b, l, HG_HEADS, HG_KEY_DIM), log_f, k_in,
                      hi.reshape(b, l, HG_HEADS, HG_VAL_DIM)).astype(h.dtype)
        o_hg = _rms(o_hg, hg_out_gain[layer]) * jax.nn.silu(hg.reshape(b, l, HG_HEADS, HG_VAL_DIM))
        o_hg = o_hg.reshape(b, l, HG_VWIDTH)

        y_a = o_hg @ w_branch_a[layer]
        y_b = o_fox @ w_branch_b[layer]
        merged = jax.nn.sigmoid(gate_a) * y_a + jax.nn.sigmoid(gate_b) * y_b
        h = h + merged @ w_out[layer]

        hn = _rms(h, norm2_gain[layer])
        u = _causal_dwconv(hn @ w_up[layer], conv_w[layer], conv_b[layer])
        u_gate, u_val = jnp.split(u, 2, axis=-1)
        h = h + (jax.nn.silu(u_gate) * u_val) @ w_down[layer]

    return h[:, N_META:]
```

```python
import functools

import jax
import jax.numpy as jnp
from jax import lax
from jax.experimental import pallas as pl
from jax.experimental.pallas import tpu as pltpu

F32 = jnp.float32
BF16 = jnp.bfloat16

D_MODEL = 1024
N_META = 16
FOX_HEADS = 8
FOX_HEAD_DIM = 64
FOX_WIDTH = FOX_HEADS * FOX_HEAD_DIM
HG_HEADS = 4
HG_DIM = 128
HG_WIDTH = HG_HEADS * HG_DIM
D_FF = 2816
CONV_WIDTH = 3
EPS = 1e-6

LANES = 128
SUBLANES = 8
LEAD = 128
PAD_ROWS = LEAD - N_META
ROW_TILE = 384
ATT_BLOCK = 256
HEADS_PER_GROUP = ATT_BLOCK // FOX_HEAD_DIM
HG_CHUNK = 128
HG_BASE = 16
FF_CHUNK = 256
MASK_VALUE = -1e30

C_FQ, C_FK, C_FV, C_HQ, C_HF, C_HI, C_HG, C_GA, C_GB, C_FF, C_END = (
    0, 512, 1024, 1536, 2048, 2560, 3072, 3584, 4608, 5632, 5760)

NT_DIMS = (((1,), (1,)), ((), ()))


def _sigmoid(x):
    return 1.0 / (1.0 + jnp.exp(-x))


def _rms_rows(x, gain):
    ms = jnp.mean(x * x, axis=-1, keepdims=True)
    return x * lax.rsqrt(ms + EPS) * gain


def _resident(shape):
    nd = len(shape)
    return pl.BlockSpec(shape, lambda *_: (0,) * nd, pipeline_mode=pl.Buffered(1))


def _proj_kernel(h_ref, g1_ref, w_ref, bf_ref, gq_ref, gk_ref, lb_ref, gm_ref,
                 q_ref, k_ref, v_ref, cum_ref, hq_ref, kin_ref, hi_ref, hgs_ref,
                 glog_ref, ga_ref, gb_ref, carry_ref, *, tiles_per_seq):
    tm = h_ref.shape[0]

    @pl.when(pl.program_id(0) % tiles_per_seq == 0)
    def _():
        carry_ref[...] = jnp.zeros_like(carry_ref)

    xn = _rms_rows(h_ref[...], g1_ref[...]).astype(BF16)

    def proj(a, b):
        return jnp.dot(xn, w_ref[:, a:b], preferred_element_type=F32)

    gm = gm_ref[...]

    def head_norm(y, gain):
        msq = jnp.dot((y * y).astype(BF16), gm, preferred_element_type=F32)
        return y * lax.rsqrt(msq + EPS) * gain

    q_ref[...] = head_norm(proj(C_FQ, C_FK), gq_ref[...]).astype(BF16)
    k_ref[...] = head_norm(proj(C_FK, C_FV), gk_ref[...]).astype(BF16)
    v_ref[...] = proj(C_FV, C_HQ).astype(BF16)

    z = proj(C_FF, C_END) + bf_ref[...]
    logf = -(jnp.maximum(-z, 0.0) + jnp.log(1.0 + jnp.exp(-jnp.abs(z))))
    p0 = logf.astype(BF16)
    r0 = logf - p0.astype(F32)
    p1 = r0.astype(BF16)
    p2 = (r0 - p1.astype(F32)).astype(BF16)
    row = lax.broadcasted_iota(jnp.int32, (tm, tm), 0)
    col = lax.broadcasted_iota(jnp.int32, (tm, tm), 1)
    tri = jnp.where(col <= row, 1.0, 0.0).astype(BF16)
    cum = (jnp.dot(tri, p0, preferred_element_type=F32)
           + jnp.dot(tri, p1, preferred_element_type=F32)
           + jnp.dot(tri, p2, preferred_element_type=F32)
           + carry_ref[...])
    cum_ref[...] = cum[:, :FOX_HEADS]
    carry_ref[...] = cum[tm - 1:tm, :]

    hq_ref[...] = proj(C_HQ, C_HF).astype(BF16)
    hf = proj(C_HF, C_HI)
    t = jnp.exp(-jnp.abs(hf))
    r = 1.0 / (1.0 + t)
    tr = t * r
    pos = hf >= 0.0
    sig_pos = jnp.where(pos, r, tr)
    sig_neg = jnp.where(pos, tr, r)
    lb = lb_ref[...]
    glog_ref[...] = jnp.log(lb + (1.0 - lb) * sig_pos)
    kin_ref[...] = ((1.0 - lb) * sig_neg).astype(BF16)
    hi_ref[...] = proj(C_HI, C_HG).astype(BF16)
    hg = proj(C_HG, C_GA)
    hgs_ref[...] = (hg * _sigmoid(hg)).astype(BF16)

    ga_ref[...] = _sigmoid(proj(C_GA, C_GB)).astype(BF16)
    gb_ref[...] = _sigmoid(proj(C_GB, C_FF)).astype(BF16)


def _attn_kernel(q_ref, k_ref, v_ref, cc_ref, crm_ref, crr_ref, o_ref,
                 qm_ref, acc_ref, m_ref, l_ref, *, n_real):
    hp = HEADS_PER_GROUP
    lane_head = lax.broadcasted_iota(jnp.int32, (1, ATT_BLOCK), 1) // FOX_HEAD_DIM

    def setup(qs, tq):
        qb = q_ref[pl.ds(qs, tq), :]
        for j in range(hp):
            qm_ref[j, :tq, :] = jnp.where(lane_head == j, qb, jnp.zeros_like(qb))
            m_ref[j, :tq, :] = jnp.full((tq, 1), MASK_VALUE, F32)
            l_ref[j, :tq, :] = jnp.zeros((tq, 1), F32)
            acc_ref[j, :tq, :] = jnp.zeros((tq, ATT_BLOCK), F32)

    def block(qs, tq, ks, tk, cr, mask):
        kb = k_ref[pl.ds(ks, tk), :]
        vb = v_ref[pl.ds(ks, tk), :]
        cq = cc_ref[pl.ds(qs, tq), :]
        for j in range(hp):
            s = lax.dot_general(qm_ref[j, :tq, :], kb, NT_DIMS,
                                preferred_element_type=F32)
            s = s + (cq[:, j:j + 1] - cr[j:j + 1, :])
            if mask is not None:
                s = jnp.where(mask, s, MASK_VALUE)
            m_old = m_ref[j, :tq, :]
            m_new = jnp.maximum(m_old, jnp.max(s, axis=-1, keepdims=True))
            alpha = jnp.exp(m_old - m_new)
            p = jnp.exp(s - m_new)
            l_ref[j, :tq, :] = alpha * l_ref[j, :tq, :] + jnp.sum(p, axis=-1, keepdims=True)
            acc_ref[j, :tq, :] = alpha * acc_ref[j, :tq, :] + jnp.dot(
                p.astype(BF16), vb, preferred_element_type=F32)
            m_ref[j, :tq, :] = m_new

    def finalize(qs, tq):
        out = jnp.zeros((tq, ATT_BLOCK), F32)
        for j in range(hp):
            out = jnp.where(lane_head == j, acc_ref[j, :tq, :] / l_ref[j, :tq, :], out)
        o_ref[pl.ds(qs, tq), :] = out.astype(BF16)

    row_m = lax.broadcasted_iota(jnp.int32, (LEAD, LEAD), 0)
    col_m = lax.broadcasted_iota(jnp.int32, (LEAD, LEAD), 1)
    setup(0, LEAD)
    block(0, LEAD, 0, LEAD, crm_ref[...], (col_m >= PAD_ROWS) & (col_m <= row_m))
    finalize(0, LEAD)

    tq = ATT_BLOCK
    lead_mask = lax.broadcasted_iota(jnp.int32, (tq, LEAD), 1) >= PAD_ROWS
    row_d = lax.broadcasted_iota(jnp.int32, (tq, tq), 0)
    col_d = lax.broadcasted_iota(jnp.int32, (tq, tq), 1)
    causal = col_d <= row_d

    def q_step(i, carry):
        qs = pl.multiple_of(LEAD + i * tq, LANES)
        setup(qs, tq)
        block(qs, tq, 0, LEAD, crm_ref[...], lead_mask)

        def k_step(jb, c):
            ks = pl.multiple_of(LEAD + jb * tq, LANES)
            block(qs, tq, ks, tq, crr_ref[jb], None)
            return c

        lax.fori_loop(0, i, k_step, 0)
        block(qs, tq, qs, tq, crr_ref[i], causal)
        finalize(qs, tq)
        return carry

    lax.fori_loop(0, n_real, q_step, 0)


def _hgrn_kernel(hq_ref, kin_ref, hi_ref, hgs_ref, gl_ref, gain_ref, o_ref,
                 st_ref, *, n_chunk):
    c_len = HG_CHUNK
    st_ref[...] = jnp.zeros_like(st_ref)
    row = lax.broadcasted_iota(jnp.int32, (c_len, c_len), 0)
    col = lax.broadcasted_iota(jnp.int32, (c_len, c_len), 1)
    rowv = lax.broadcasted_iota(jnp.int32, (c_len, HG_DIM), 0)
    gain = gain_ref[...]

    def ref_rows(g, period, offset):
        parts = [jnp.broadcast_to(g[a * period + offset:a * period + offset + 1, :],
                                  (period, HG_DIM))
                 for a in range(c_len // period)]
        return jnp.concatenate(parts, axis=0)

    def body(c, carry):
        rows = pl.ds(pl.multiple_of(c * c_len, c_len), c_len)
        g = gl_ref[rows, :]
        d = 1
        while d < c_len:
            g = g + jnp.where(rowv >= d, pltpu.roll(g, d, 0), 0.0)
            d *= 2
        hq = hq_ref[rows, :].astype(F32)
        kin = kin_ref[rows, :].astype(F32)
        v = hi_ref[rows, :]

        a_mat = jnp.zeros((c_len, c_len), F32)
        bs = c_len // 2
        while bs >= HG_BASE:
            ref = ref_rows(g, 2 * bs, bs - 1)
            ql = (hq * jnp.exp(jnp.minimum(g - ref, 0.0))).astype(BF16)
            kl = (kin * jnp.exp(jnp.minimum(ref - g, 0.0))).astype(BF16)
            al = lax.dot_general(ql, kl, NT_DIMS, preferred_element_type=F32)
            same = (row // (2 * bs)) == (col // (2 * bs))
            ml = same & ((row & bs) != 0) & ((col & bs) == 0)
            a_mat = jnp.where(ml, al, a_mat)
            bs //= 2
        ref = ref_rows(g, HG_BASE, HG_BASE // 2 - 1)
        qd = (hq * jnp.exp(g - ref)).astype(BF16)
        kd = (kin * jnp.exp(ref - g)).astype(BF16)
        ad = lax.dot_general(qd, kd, NT_DIMS, preferred_element_type=F32)
        md = ((row // HG_BASE) == (col // HG_BASE)) & (col <= row)
        a_mat = jnp.where(md, ad, a_mat)

        st = st_ref[...]
        o = jnp.dot(a_mat.astype(BF16), v, preferred_element_type=F32)
        o = o + lax.dot_general((hq * jnp.exp(g)).astype(BF16), st.astype(BF16),
                                NT_DIMS, preferred_element_type=F32)

        g_tot = g[c_len - 1:c_len, :]
        kt = (kin * jnp.exp(g_tot - g)).astype(BF16)
        v_t = v.astype(F32).T.astype(BF16)
        st_ref[...] = st * jnp.exp(g_tot) + jnp.dot(v_t, kt, preferred_element_type=F32)

        y = _rms_rows(o, gain) * hgs_ref[rows, :].astype(F32)
        o_ref[rows, :] = y.astype(BF16)
        return carry

    lax.fori_loop(0, n_chunk, body, 0)


def _mix_kernel(h_ref, ohg_ref, ofox_ref, ga_ref, gb_ref, wa_ref, wb_ref, wo_ref, o_ref):
    y_a = jnp.dot(ohg_ref[...], wa_ref[...], preferred_element_type=F32)
    y_b = jnp.dot(ofox_ref[...], wb_ref[...], preferred_element_type=F32)
    merged = ga_ref[...].astype(F32) * y_a + gb_ref[...].astype(F32) * y_b
    o_ref[...] = h_ref[...] + jnp.dot(merged.astype(BF16), wo_ref[...],
                                      preferred_element_type=F32)


def _ffn_kernel(h_ref, g2_ref, wup_ref, cw_ref, cb_ref, wdn_ref, o_ref, carry_ref):
    tm = h_ref.shape[0]

    @pl.when(pl.program_id(0) == 0)
    def _():
        carry_ref[...] = jnp.zeros_like(carry_ref)

    h = h_ref[...]
    hn = _rms_rows(h, g2_ref[...]).astype(BF16)
    rowi = lax.broadcasted_iota(jnp.int32, (tm, FF_CHUNK), 0)

    def conv_cols(lo):
        cols = slice(lo, lo + FF_CHUNK)
        u = jnp.dot(hn, wup_ref[:, cols], preferred_element_type=F32)
        prev = carry_ref[:, cols]
        p1 = prev[SUBLANES - 1:SUBLANES, :]
        p2 = prev[SUBLANES - 2:SUBLANES - 1, :]
        s1 = jnp.where(rowi == 0, p1, pltpu.roll(u, 1, 0))
        s2 = jnp.where(rowi == 0, p2, jnp.where(rowi == 1, p1, pltpu.roll(u, 2, 0)))
        carry_ref[:, cols] = u[tm - SUBLANES:tm, :]
        cw = cw_ref[:, cols]
        return cw[0:1, :] * s2 + cw[1:2, :] * s1 + cw[2:3, :] * u + cb_ref[:, cols]

    acc = h
    for c in range(D_FF // FF_CHUNK):
        u_gate = conv_cols(c * FF_CHUNK)
        u_val = conv_cols(D_FF + c * FF_CHUNK)
        act = (u_gate * _sigmoid(u_gate) * u_val).astype(BF16)
        acc = acc + jnp.dot(act, wdn_ref[c * FF_CHUNK:(c + 1) * FF_CHUNK, :],
                            preferred_element_type=F32)
    o_ref[...] = acc


def _row_spec(width, tm=ROW_TILE):
    return pl.BlockSpec((tm, width), lambda i: (i, 0))


def kernel(x, meta_tokens, norm1_gain, w_in, fox_b_f, q_norm_gain, k_norm_gain,
           hg_lb_logits, hg_out_gain, w_branch_a, w_branch_b, w_out, norm2_gain,
           w_up, conv_w, conv_b, w_down):
    batch, seq, d = x.shape
    lp = LEAD + seq
    rows = batch * lp
    assert d == D_MODEL and lp % ROW_TILE == 0 and seq % ATT_BLOCK == 0
    assert lp % HG_CHUNK == 0 and D_FF % FF_CHUNK == 0
    n_tiles = rows // ROW_TILE
    layer = 0

    hp = jnp.concatenate([
        jnp.zeros((batch, PAD_ROWS, d), F32),
        jnp.broadcast_to(meta_tokens[None].astype(F32), (batch, N_META, d)),
        x.astype(F32)], axis=1).reshape(rows, d)

    w = w_in[layer]
    o_fq, o_fk, o_fv, o_ff = 0, 512, 1024, 1536
    o_hq = o_ff + FOX_HEADS
    o_hf, o_hi, o_hg = o_hq + 512, o_hq + 1024, o_hq + 1536
    o_ga = o_hq + 2048
    o_gb = o_ga + D_MODEL
    w1 = jnp.concatenate([
        w[:, o_fq:o_fk], w[:, o_fk:o_fv], w[:, o_fv:o_ff],
        w[:, o_hq:o_hf], w[:, o_hf:o_hi], w[:, o_hi:o_hg], w[:, o_hg:o_ga],
        w[:, o_ga:o_gb], w[:, o_gb:o_gb + D_MODEL],
        w[:, o_ff:o_hq], jnp.zeros((d, LANES - FOX_HEADS), w.dtype)], axis=1).astype(BF16)
    bf = jnp.concatenate([fox_b_f[layer].astype(F32),
                          jnp.zeros((LANES - FOX_HEADS,), F32)]).reshape(1, LANES)
    scale = FOX_HEAD_DIM ** -0.5
    gq = (jnp.tile(q_norm_gain[layer].astype(F32), FOX_HEADS) * scale).reshape(1, FOX_WIDTH)
    gk = jnp.tile(k_norm_gain[layer].astype(F32), FOX_HEADS).reshape(1, FOX_WIDTH)
    lower_bounds = jnp.cumsum(jax.nn.softmax(hg_lb_logits.astype(F32), axis=0), axis=0)
    lb = lower_bounds[layer].reshape(1, HG_WIDTH)
    head_id = jnp.arange(FOX_WIDTH) // FOX_HEAD_DIM
    gm = jnp.where(head_id[:, None] == head_id[None, :], 1.0 / FOX_HEAD_DIM, 0.0).astype(BF16)

    bf16_w = lambda width: jax.ShapeDtypeStruct((rows, width), BF16)
    outs = pl.pallas_call(
        functools.partial(_proj_kernel, tiles_per_seq=lp // ROW_TILE),
        grid=(n_tiles,),
        in_specs=[_row_spec(d), _resident((1, d)), _resident((d, C_END)),
                  _resident((1, LANES)), _resident((1, FOX_WIDTH)), _resident((1, FOX_WIDTH)),
                  _resident((1, HG_WIDTH)), _resident((FOX_WIDTH, FOX_WIDTH))],
        out_specs=[_row_spec(FOX_WIDTH), _row_spec(FOX_WIDTH), _row_spec(FOX_WIDTH),
                   _row_spec(FOX_HEADS),
                   _row_spec(HG_WIDTH), _row_spec(HG_WIDTH), _row_spec(HG_WIDTH),
                   _row_spec(HG_WIDTH), _row_spec(HG_WIDTH),
                   _row_spec(d), _row_spec(d)],
        out_shape=[bf16_w(FOX_WIDTH), bf16_w(FOX_WIDTH), bf16_w(FOX_WIDTH),
                   jax.ShapeDtypeStruct((rows, FOX_HEADS), F32),
                   bf16_w(HG_WIDTH), bf16_w(HG_WIDTH), bf16_w(HG_WIDTH), bf16_w(HG_WIDTH),
                   jax.ShapeDtypeStruct((rows, HG_WIDTH), F32),
                   bf16_w(d), bf16_w(d)],
        scratch_shapes=[pltpu.VMEM((1, LANES), F32)],
        compiler_params=pltpu.CompilerParams(
            dimension_semantics=("arbitrary",), vmem_limit_bytes=52 * 1024 * 1024),
        name="proj",
    )(hp, norm1_gain[layer].astype(F32).reshape(1, d), w1, bf, gq, gk, lb, gm)
    fq, fk, fv, cum, hq, kin, hi, hgs, glog, ga, gb = outs

    n_grp = FOX_HEADS // HEADS_PER_GROUP
    n_real = seq // ATT_BLOCK
    cum3 = cum.reshape(batch, lp, n_grp, HEADS_PER_GROUP)
    cum_col = jnp.transpose(cum3, (0, 2, 1, 3))
    cum_row = jnp.transpose(cum3, (0, 2, 3, 1))
    cum_row_lead = cum_row[..., :LEAD]
    cum_row_real = jnp.transpose(
        cum_row[..., LEAD:].reshape(batch, n_grp, HEADS_PER_GROUP, n_real, ATT_BLOCK),
        (0, 1, 3, 2, 4))
    seq_spec = pl.BlockSpec((lp, ATT_BLOCK), lambda b, g: (b, g))
    o_fox = pl.pallas_call(
        functools.partial(_attn_kernel, n_real=n_real),
        grid=(batch, n_grp),
        in_specs=[seq_spec, seq_spec, seq_spec,
                  pl.BlockSpec((None, None, lp, HEADS_PER_GROUP), lambda b, g: (b, g, 0, 0)),
                  pl.BlockSpec((None, None, HEADS_PER_GROUP, LEAD), lambda b, g: (b, g, 0, 0)),
                  pl.BlockSpec((None, None, n_real, HEADS_PER_GROUP, ATT_BLOCK),
                               lambda b, g: (b, g, 0, 0, 0))],
        out_specs=seq_spec,
        out_shape=jax.ShapeDtypeStruct((rows, FOX_WIDTH), BF16),
        scratch_shapes=[pltpu.VMEM((HEADS_PER_GROUP, ATT_BLOCK, ATT_BLOCK), BF16),
                        pltpu.VMEM((HEADS_PER_GROUP, ATT_BLOCK, ATT_BLOCK), F32),
                        pltpu.VMEM((HEADS_PER_GROUP, ATT_BLOCK, 1), F32),
                        pltpu.VMEM((HEADS_PER_GROUP, ATT_BLOCK, 1), F32)],
        compiler_params=pltpu.CompilerParams(
            dimension_semantics=("parallel", "parallel"), vmem_limit_bytes=48 * 1024 * 1024),
        name="fox_attn",
    )(fq, fk, fv, cum_col, cum_row_lead, cum_row_real)

    head_spec = pl.BlockSpec((lp, HG_DIM), lambda b, hh: (b, hh))
    o_hg = pl.pallas_call(
        functools.partial(_hgrn_kernel, n_chunk=lp // HG_CHUNK),
        grid=(batch, HG_HEADS),
        in_specs=[head_spec, head_spec, head_spec, head_spec, head_spec,
                  pl.BlockSpec((1, HG_DIM), lambda b, hh: (0, 0))],
        out_specs=head_spec,
        out_shape=jax.ShapeDtypeStruct((rows, HG_WIDTH), BF16),
        scratch_shapes=[pltpu.VMEM((HG_DIM, HG_DIM), F32)],
        compiler_params=pltpu.CompilerParams(
            dimension_semantics=("parallel", "parallel"), vmem_limit_bytes=40 * 1024 * 1024),
        name="hgrn2",
    )(hq, kin, hi, hgs, glog, hg_out_gain[layer].astype(F32).reshape(1, HG_DIM))

    h1 = pl.pallas_call(
        _mix_kernel,
        grid=(n_tiles,),
        in_specs=[_row_spec(d), _row_spec(HG_WIDTH), _row_spec(FOX_WIDTH),
                  _row_spec(d), _row_spec(d),
                  _resident((HG_WIDTH, d)), _resident((FOX_WIDTH, d)), _resident((d, d))],
        out_specs=_row_spec(d),
        out_shape=jax.ShapeDtypeStruct((rows, d), F32),
        compiler_params=pltpu.CompilerParams(
            dimension_semantics=("parallel",), vmem_limit_bytes=40 * 1024 * 1024),
        name="mix",
    )(hp, o_hg, o_fox, ga, gb, w_branch_a[layer].astype(BF16),
      w_branch_b[layer].astype(BF16), w_out[layer].astype(BF16))

    h2 = pl.pallas_call(
        _ffn_kernel,
        grid=(n_tiles,),
        in_specs=[_row_spec(d), _resident((1, d)), _resident((d, 2 * D_FF)),
                  _resident((CONV_WIDTH, 2 * D_FF)), _resident((1, 2 * D_FF)),
                  _resident((D_FF, d))],
        out_specs=_row_spec(d),
        out_shape=jax.ShapeDtypeStruct((rows, d), F32),
        scratch_shapes=[pltpu.VMEM((SUBLANES, 2 * D_FF), F32)],
        compiler_params=pltpu.CompilerParams(
            dimension_semantics=("arbitrary",), vmem_limit_bytes=52 * 1024 * 1024),
        name="ffn",
    )(h1, norm2_gain[layer].astype(F32).reshape(1, d), w_up[layer].astype(BF16),
      conv_w[layer].astype(F32), conv_b[layer].astype(F32).reshape(1, 2 * D_FF),
      w_down[layer].astype(BF16))

    return h2.reshape(batch, lp, d)[:, LEAD:, :].astype(x.dtype)
```

```python
import functools

import jax
import jax.numpy as jnp
from jax import lax
from jax.experimental import pallas as pl
from jax.experimental.pallas import tpu as pltpu

F32 = jnp.float32
BF16 = jnp.bfloat16

D_MODEL = 1024
N_META = 16
FOX_HEADS = 8
FOX_HEAD_DIM = 64
FOX_WIDTH = FOX_HEADS * FOX_HEAD_DIM
HG_HEADS = 4
HG_DIM = 128
HG_WIDTH = HG_HEADS * HG_DIM
D_FF = 2816
CONV_WIDTH = 3
EPS = 1e-6

LANES = 128
SUBLANES = 8
LEAD = 128
PAD_ROWS = LEAD - N_META
ROW_TILE = 384
GROUP_LANES = 256
HEADS_PER_GROUP = GROUP_LANES // FOX_HEAD_DIM
ATT_BLOCK = 1024
KEY_BLOCK = 256
KEY_UNROLL = 2
LOG2E = 1.4426950408889634
HG_CHUNK = 128
HG_BASE = 16
FF_CHUNK = 256
MASK_VALUE = -1e30

C_FQ, C_FK, C_FV, C_HQ, C_HF, C_HI, C_HG, C_GA, C_GB, C_FF, C_END = (
    0, 512, 1024, 1536, 2048, 2560, 3072, 3584, 4608, 5632, 5760)

NT_DIMS = (((1,), (1,)), ((), ()))


def _sigmoid(x):
    return 1.0 / (1.0 + jnp.exp(-x))


def _rms_rows(x, gain):
    ms = jnp.mean(x * x, axis=-1, keepdims=True)
    return x * lax.rsqrt(ms + EPS) * gain


def _resident(shape):
    nd = len(shape)
    return pl.BlockSpec(shape, lambda *_: (0,) * nd, pipeline_mode=pl.Buffered(1))


def _proj_kernel(h_ref, g1_ref, w_ref, bf_ref, gq_ref, gk_ref, lb_ref, gm_ref,
                 q_ref, k_ref, v_ref, cum_ref, hq_ref, kin_ref, hi_ref, hgs_ref,
                 glog_ref, ga_ref, gb_ref, carry_ref, *, tiles_per_seq):
    tm = h_ref.shape[0]

    @pl.when(pl.program_id(0) % tiles_per_seq == 0)
    def _():
        carry_ref[...] = jnp.zeros_like(carry_ref)

    xn = _rms_rows(h_ref[...], g1_ref[...]).astype(BF16)

    def proj(a, b):
        return jnp.dot(xn, w_ref[:, a:b], preferred_element_type=F32)

    gm = gm_ref[...]

    def head_norm(y, gain):
        msq = jnp.dot((y * y).astype(BF16), gm, preferred_element_type=F32)
        return y * lax.rsqrt(msq + EPS) * gain

    q_ref[...] = head_norm(proj(C_FQ, C_FK), gq_ref[...]).astype(BF16)
    k_ref[...] = head_norm(proj(C_FK, C_FV), gk_ref[...]).astype(BF16)
    v_ref[...] = proj(C_FV, C_HQ).astype(BF16)

    z = proj(C_FF, C_END) + bf_ref[...]
    logf = -(jnp.maximum(-z, 0.0) + jnp.log(1.0 + jnp.exp(-jnp.abs(z))))
    p0 = logf.astype(BF16)
    r0 = logf - p0.astype(F32)
    p1 = r0.astype(BF16)
    p2 = (r0 - p1.astype(F32)).astype(BF16)
    row = lax.broadcasted_iota(jnp.int32, (tm, tm), 0)
    col = lax.broadcasted_iota(jnp.int32, (tm, tm), 1)
    tri = jnp.where(col <= row, 1.0, 0.0).astype(BF16)
    cum = (jnp.dot(tri, p0, preferred_element_type=F32)
           + jnp.dot(tri, p1, preferred_element_type=F32)
           + jnp.dot(tri, p2, preferred_element_type=F32)
           + carry_ref[...])
    cum_ref[...] = cum[:, :FOX_HEADS]
    carry_ref[...] = cum[tm - 1:tm, :]

    hq_ref[...] = proj(C_HQ, C_HF).astype(BF16)
    hf = proj(C_HF, C_HI)
    t = jnp.exp(-jnp.abs(hf))
    r = 1.0 / (1.0 + t)
    tr = t * r
    pos = hf >= 0.0
    sig_pos = jnp.where(pos, r, tr)
    sig_neg = jnp.where(pos, tr, r)
    lb = lb_ref[...]
    glog_ref[...] = jnp.log(lb + (1.0 - lb) * sig_pos)
    kin_ref[...] = ((1.0 - lb) * sig_neg).astype(BF16)
    hi_ref[...] = proj(C_HI, C_HG).astype(BF16)
    hg = proj(C_HG, C_GA)
    hgs_ref[...] = (hg * _sigmoid(hg)).astype(BF16)

    ga_ref[...] = _sigmoid(proj(C_GA, C_GB)).astype(BF16)
    gb_ref[...] = _sigmoid(proj(C_GB, C_FF)).astype(BF16)


def _attn_kernel(q_ref, k_ref, v_ref, crm_ref, crr_ref, o_ref,
                 qm_ref, acc_ref, m_ref, *, n_real):
    hp = HEADS_PER_GROUP
    lane_head = lax.broadcasted_iota(jnp.int32, (1, GROUP_LANES), 1) // FOX_HEAD_DIM

    def setup(qs, tq):
        qb = q_ref[pl.ds(qs, tq), :]
        for j in range(hp):
            qm_ref[j, :tq, :] = jnp.where(lane_head == j, qb, jnp.zeros_like(qb))
            m_ref[j, :tq, :] = jnp.full((tq, LANES), MASK_VALUE, F32)
            acc_ref[j, :tq, :] = jnp.zeros((tq, GROUP_LANES), F32)

    def block(r0, nr, ks, tk, c0, cr, mask):
        kb = k_ref[pl.ds(ks, tk), :]
        vb = v_ref[pl.ds(ks, tk), :]
        bias = (c0 - cr) * LOG2E
        rows = slice(r0, r0 + nr)
        for j in range(hp):
            s = lax.dot_general(qm_ref[j, rows, :], kb, NT_DIMS,
                                preferred_element_type=F32)
            s = s + bias[j:j + 1, :]
            if mask is not None:
                s = jnp.where(mask, s, MASK_VALUE)
            m_old = m_ref[j, rows, :]
            m_new = jnp.maximum(m_old, jnp.max(s, axis=-1, keepdims=True))
            alpha = jnp.exp2(m_old - m_new)
            p = jnp.exp2(s - jnp.concatenate([m_new] * (tk // LANES), axis=1))
            vj = jnp.where(lane_head == j, vb,
                           jnp.where(lane_head == (j + 1) % hp, 1.0, 0.0).astype(BF16))
            acc_ref[j, rows, :] = (
                jnp.concatenate([alpha] * (GROUP_LANES // LANES), axis=1) * acc_ref[j, rows, :]
                + jnp.dot(p.astype(BF16), vj, preferred_element_type=F32))
            m_ref[j, rows, :] = m_new

    def finalize(qs, tq):
        out = jnp.zeros((tq, GROUP_LANES), F32)
        for j in range(hp):
            acc = acc_ref[j, :tq, :]
            c = ((j + 1) % hp) * FOX_HEAD_DIM
            out = jnp.where(lane_head == j, acc * (1.0 / acc[:, c:c + 1]), out)
        o_ref[pl.ds(qs, tq), :] = out.astype(BF16)

    crm = crm_ref[...]
    row_m = lax.broadcasted_iota(jnp.int32, (LEAD, LEAD), 0)
    col_m = lax.broadcasted_iota(jnp.int32, (LEAD, LEAD), 1)
    setup(0, LEAD)
    block(0, LEAD, 0, LEAD, crm[:, PAD_ROWS:PAD_ROWS + 1], crm,
          (col_m >= PAD_ROWS) & (col_m <= row_m))
    finalize(0, LEAD)

    tq, tk = ATT_BLOCK, KEY_BLOCK
    kpq = tq // tk
    lead_mask = lax.broadcasted_iota(jnp.int32, (tq, LEAD), 1) >= PAD_ROWS

    def q_step(i, carry):
        qs = pl.multiple_of(LEAD + i * tq, LANES)
        c0 = crr_ref[i * kpq][:, 0:1]
        setup(qs, tq)
        block(0, tq, 0, LEAD, c0, crm, lead_mask)

        def k_step(jq, c):
            for d in range(KEY_UNROLL):
                jb = jq * KEY_UNROLL + d
                ks = pl.multiple_of(LEAD + jb * tk, LANES)
                block(0, tq, ks, tk, c0, crr_ref[jb], None)
            return c

        lax.fori_loop(0, i * (kpq // KEY_UNROLL), k_step, 0)
        for d in range(kpq):
            nr = tq - d * tk
            causal = (lax.broadcasted_iota(jnp.int32, (nr, tk), 1)
                      <= lax.broadcasted_iota(jnp.int32, (nr, tk), 0))
            block(d * tk, nr, pl.multiple_of(qs + d * tk, LANES), tk, c0,
                  crr_ref[i * kpq + d], causal)
        finalize(qs, tq)
        return carry

    lax.fori_loop(0, n_real, q_step, 0)


def _hgrn_kernel(hq_ref, kin_ref, hi_ref, hgs_ref, gl_ref, gain_ref, o_ref,
                 st_ref, *, n_chunk):
    c_len = HG_CHUNK
    st_ref[...] = jnp.zeros_like(st_ref)
    row = lax.broadcasted_iota(jnp.int32, (c_len, c_len), 0)
    col = lax.broadcasted_iota(jnp.int32, (c_len, c_len), 1)
    rowv = lax.broadcasted_iota(jnp.int32, (c_len, HG_DIM), 0)
    gain = gain_ref[...]

    def ref_rows(g, period, offset):
        parts = [jnp.broadcast_to(g[a * period + offset:a * period + offset + 1, :],
                                  (period, HG_DIM))
                 for a in range(c_len // period)]
        return jnp.concatenate(parts, axis=0)

    def body(c, carry):
        rows = pl.ds(pl.multiple_of(c * c_len, c_len), c_len)
        g = gl_ref[rows, :]
        d = 1
        while d < c_len:
            g = g + jnp.where(rowv >= d, pltpu.roll(g, d, 0), 0.0)
            d *= 2
        hq = hq_ref[rows, :].astype(F32)
        kin = kin_ref[rows, :].astype(F32)
        v = hi_ref[rows, :]

        a_mat = jnp.zeros((c_len, c_len), F32)
        bs = c_len // 2
        while bs >= HG_BASE:
            ref = ref_rows(g, 2 * bs, bs - 1)
            ql = (hq * jnp.exp(jnp.minimum(g - ref, 0.0))).astype(BF16)
            kl = (kin * jnp.exp(jnp.minimum(ref - g, 0.0))).astype(BF16)
            al = lax.dot_general(ql, kl, NT_DIMS, preferred_element_type=F32)
            same = (row // (2 * bs)) == (col // (2 * bs))
            ml = same & ((row & bs) != 0) & ((col & bs) == 0)
            a_mat = jnp.where(ml, al, a_mat)
            bs //= 2
        ref = ref_rows(g, HG_BASE, HG_BASE // 2 - 1)
        qd = (hq * jnp.exp(g - ref)).astype(BF16)
        kd = (kin * jnp.exp(ref - g)).astype(BF16)
        ad = lax.dot_general(qd, kd, NT_DIMS, preferred_element_type=F32)
        md = ((row // HG_BASE) == (col // HG_BASE)) & (col <= row)
        a_mat = jnp.where(md, ad, a_mat)

        st = st_ref[...]
        o = jnp.dot(a_mat.astype(BF16), v, preferred_element_type=F32)
        o = o + lax.dot_general((hq * jnp.exp(g)).astype(BF16), st.astype(BF16),
                                NT_DIMS, preferred_element_type=F32)

        g_tot = g[c_len - 1:c_len, :]
        kt = (kin * jnp.exp(g_tot - g)).astype(BF16)
        v_t = v.astype(F32).T.astype(BF16)
        st_ref[...] = st * jnp.exp(g_tot) + jnp.dot(v_t, kt, preferred_element_type=F32)

        y = _rms_rows(o, gain) * hgs_ref[rows, :].astype(F32)
        o_ref[rows, :] = y.astype(BF16)
        return carry

    lax.fori_loop(0, n_chunk, body, 0)


def _mix_kernel(h_ref, ohg_ref, ofox_ref, ga_ref, gb_ref, wa_ref, wb_ref, wo_ref, o_ref):
    y_a = jnp.dot(ohg_ref[...], wa_ref[...], preferred_element_type=F32)
    y_b = jnp.dot(ofox_ref[...], wb_ref[...], preferred_element_type=F32)
    merged = ga_ref[...].astype(F32) * y_a + gb_ref[...].astype(F32) * y_b
    o_ref[...] = h_ref[...] + jnp.dot(merged.astype(BF16), wo_ref[...],
                                      preferred_element_type=F32)


def _ffn_kernel(h_ref, g2_ref, wup_ref, cw_ref, cb_ref, wdn_ref, o_ref, carry_ref, perm_ref):
    tm = h_ref.shape[0]
    nv = tm // SUBLANES

    @pl.when(pl.program_id(0) == 0)
    def _():
        carry_ref[...] = jnp.zeros_like(carry_ref)

    n_slab = h_ref.shape[1] // LANES
    for j in range(n_slab):
        perm_ref[j] = h_ref[:, j * LANES:(j + 1) * LANES]
    h = jnp.concatenate(
        [jnp.concatenate([perm_ref[j, pl.ds(k, SUBLANES, stride=nv), :] for k in range(nv)],
                         axis=0) for j in range(n_slab)], axis=1)
    hn = _rms_rows(h, g2_ref[...]).astype(BF16)
    sub = lax.broadcasted_iota(jnp.int32, (SUBLANES, FF_CHUNK), 0)

    def up_cols(lo):
        return jnp.dot(hn, wup_ref[:, lo:lo + FF_CHUNK], preferred_element_type=F32)

    def conv_cols(u, lo):
        cols = slice(lo, lo + FF_CHUNK)
        prev = carry_ref[:, cols]
        carry_ref[:, cols] = u[tm - 2 * SUBLANES:tm, :]

        def wrap(prev8, last8):
            return jnp.where(sub == 0, pltpu.roll(prev8, 1, 0), pltpu.roll(last8, 1, 0))

        f1 = wrap(prev[SUBLANES:, :], u[tm - SUBLANES:tm, :])
        f2 = wrap(prev[:SUBLANES, :], u[tm - 2 * SUBLANES:tm - SUBLANES, :])
        s1 = jnp.concatenate([f1, u[:tm - SUBLANES, :]], axis=0)
        s2 = jnp.concatenate([f2, f1, u[:tm - 2 * SUBLANES, :]], axis=0)
        cw = cw_ref[:, cols]
        return cw[0:1, :] * s2 + cw[1:2, :] * s1 + cw[2:3, :] * u + cb_ref[:, cols]

    n_chunks = D_FF // FF_CHUNK
    nxt = (up_cols(0), up_cols(D_FF))
    acc = h
    for c in range(n_chunks):
        u_g, u_v = nxt
        if c + 1 < n_chunks:
            nxt = (up_cols((c + 1) * FF_CHUNK), up_cols(D_FF + (c + 1) * FF_CHUNK))
        u_gate = conv_cols(u_g, c * FF_CHUNK)
        u_val = conv_cols(u_v, D_FF + c * FF_CHUNK)
        act = (u_gate * _sigmoid(u_gate) * u_val).astype(BF16)
        acc = acc + jnp.dot(act, wdn_ref[c * FF_CHUNK:(c + 1) * FF_CHUNK, :],
                            preferred_element_type=F32)
    for j in range(n_slab):
        for k in range(nv):
            perm_ref[j, pl.ds(k, SUBLANES, stride=nv), :] = acc[
                k * SUBLANES:(k + 1) * SUBLANES, j * LANES:(j + 1) * LANES]
        o_ref[:, j * LANES:(j + 1) * LANES] = perm_ref[j]


def _row_spec(width, tm=ROW_TILE):
    return pl.BlockSpec((tm, width), lambda i: (i, 0))


def kernel(x, meta_tokens, norm1_gain, w_in, fox_b_f, q_norm_gain, k_norm_gain,
           hg_lb_logits, hg_out_gain, w_branch_a, w_branch_b, w_out, norm2_gain,
           w_up, conv_w, conv_b, w_down):
    batch, seq, d = x.shape
    lp = LEAD + seq
    rows = batch * lp
    assert d == D_MODEL and lp % ROW_TILE == 0 and seq % ATT_BLOCK == 0
    assert lp % HG_CHUNK == 0 and D_FF % FF_CHUNK == 0
    n_tiles = rows // ROW_TILE
    layer = 0

    hp = jnp.concatenate([
        jnp.zeros((batch, PAD_ROWS, d), F32),
        jnp.broadcast_to(meta_tokens[None].astype(F32), (batch, N_META, d)),
        x.astype(F32)], axis=1).reshape(rows, d)

    w = w_in[layer]
    o_fq, o_fk, o_fv, o_ff = 0, 512, 1024, 1536
    o_hq = o_ff + FOX_HEADS
    o_hf, o_hi, o_hg = o_hq + 512, o_hq + 1024, o_hq + 1536
    o_ga = o_hq + 2048
    o_gb = o_ga + D_MODEL
    w1 = jnp.concatenate([
        w[:, o_fq:o_fk], w[:, o_fk:o_fv], w[:, o_fv:o_ff],
        w[:, o_hq:o_hf], w[:, o_hf:o_hi], w[:, o_hi:o_hg], w[:, o_hg:o_ga],
        w[:, o_ga:o_gb], w[:, o_gb:o_gb + D_MODEL],
        w[:, o_ff:o_hq], jnp.zeros((d, LANES - FOX_HEADS), w.dtype)], axis=1).astype(BF16)
    bf = jnp.concatenate([fox_b_f[layer].astype(F32),
                          jnp.zeros((LANES - FOX_HEADS,), F32)]).reshape(1, LANES)
    scale = FOX_HEAD_DIM ** -0.5 * LOG2E
    gq = (jnp.tile(q_norm_gain[layer].astype(F32), FOX_HEADS) * scale).reshape(1, FOX_WIDTH)
    gk = jnp.tile(k_norm_gain[layer].astype(F32), FOX_HEADS).reshape(1, FOX_WIDTH)
    lower_bounds = jnp.cumsum(jax.nn.softmax(hg_lb_logits.astype(F32), axis=0), axis=0)
    lb = lower_bounds[layer].reshape(1, HG_WIDTH)
    head_id = jnp.arange(FOX_WIDTH) // FOX_HEAD_DIM
    gm = jnp.where(head_id[:, None] == head_id[None, :], 1.0 / FOX_HEAD_DIM, 0.0).astype(BF16)

    bf16_w = lambda width: jax.ShapeDtypeStruct((rows, width), BF16)
    outs = pl.pallas_call(
        functools.partial(_proj_kernel, tiles_per_seq=lp // ROW_TILE),
        grid=(n_tiles,),
        in_specs=[_row_spec(d), _resident((1, d)), _resident((d, C_END)),
                  _resident((1, LANES)), _resident((1, FOX_WIDTH)), _resident((1, FOX_WIDTH)),
                  _resident((1, HG_WIDTH)), _resident((FOX_WIDTH, FOX_WIDTH))],
        out_specs=[_row_spec(FOX_WIDTH), _row_spec(FOX_WIDTH), _row_spec(FOX_WIDTH),
                   _row_spec(FOX_HEADS),
                   _row_spec(HG_WIDTH), _row_spec(HG_WIDTH), _row_spec(HG_WIDTH),
                   _row_spec(HG_WIDTH), _row_spec(HG_WIDTH),
                   _row_spec(d), _row_spec(d)],
        out_shape=[bf16_w(FOX_WIDTH), bf16_w(FOX_WIDTH), bf16_w(FOX_WIDTH),
                   jax.ShapeDtypeStruct((rows, FOX_HEADS), F32),
                   bf16_w(HG_WIDTH), bf16_w(HG_WIDTH), bf16_w(HG_WIDTH), bf16_w(HG_WIDTH),
                   jax.ShapeDtypeStruct((rows, HG_WIDTH), F32),
                   bf16_w(d), bf16_w(d)],
        scratch_shapes=[pltpu.VMEM((1, LANES), F32)],
        compiler_params=pltpu.CompilerParams(
            dimension_semantics=("arbitrary",), vmem_limit_bytes=52 * 1024 * 1024),
        name="proj",
    )(hp, norm1_gain[layer].astype(F32).reshape(1, d), w1, bf, gq, gk, lb, gm)
    fq, fk, fv, cum, hq, kin, hi, hgs, glog, ga, gb = outs

    n_grp = FOX_HEADS // HEADS_PER_GROUP
    n_real = seq // ATT_BLOCK
    cum_row = jnp.transpose(cum.reshape(batch, lp, n_grp, HEADS_PER_GROUP), (0, 2, 3, 1))
    cum_row_lead = cum_row[..., :LEAD]
    n_kb = seq // KEY_BLOCK
    cum_row_real = jnp.transpose(
        cum_row[..., LEAD:].reshape(batch, n_grp, HEADS_PER_GROUP, n_kb, KEY_BLOCK),
        (0, 1, 3, 2, 4))
    seq_spec = pl.BlockSpec((lp, GROUP_LANES), lambda b, g: (b, g))
    o_fox = pl.pallas_call(
        functools.partial(_attn_kernel, n_real=n_real),
        grid=(batch, n_grp),
        in_specs=[seq_spec, seq_spec, seq_spec,
                  pl.BlockSpec((None, None, HEADS_PER_GROUP, LEAD), lambda b, g: (b, g, 0, 0)),
                  pl.BlockSpec((None, None, n_kb, HEADS_PER_GROUP, KEY_BLOCK),
                               lambda b, g: (b, g, 0, 0, 0))],
        out_specs=seq_spec,
        out_shape=jax.ShapeDtypeStruct((rows, FOX_WIDTH), BF16),
        scratch_shapes=[pltpu.VMEM((HEADS_PER_GROUP, ATT_BLOCK, GROUP_LANES), BF16),
                        pltpu.VMEM((HEADS_PER_GROUP, ATT_BLOCK, GROUP_LANES), F32),
                        pltpu.VMEM((HEADS_PER_GROUP, ATT_BLOCK, LANES), F32)],
        compiler_params=pltpu.CompilerParams(
            dimension_semantics=("parallel", "parallel"), vmem_limit_bytes=48 * 1024 * 1024),
        name="fox_attn",
    )(fq, fk, fv, cum_row_lead, cum_row_real)

    head_spec = pl.BlockSpec((lp, HG_DIM), lambda b, hh: (b, hh))
    o_hg = pl.pallas_call(
        functools.partial(_hgrn_kernel, n_chunk=lp // HG_CHUNK),
        grid=(batch, HG_HEADS),
        in_specs=[head_spec, head_spec, head_spec, head_spec, head_spec,
                  pl.BlockSpec((1, HG_DIM), lambda b, hh: (0, 0))],
        out_specs=head_spec,
        out_shape=jax.ShapeDtypeStruct((rows, HG_WIDTH), BF16),
        scratch_shapes=[pltpu.VMEM((HG_DIM, HG_DIM), F32)],
        compiler_params=pltpu.CompilerParams(
            dimension_semantics=("parallel", "parallel"), vmem_limit_bytes=40 * 1024 * 1024),
        name="hgrn2",
    )(hq, kin, hi, hgs, glog, hg_out_gain[layer].astype(F32).reshape(1, HG_DIM))

    h1 = pl.pallas_call(
        _mix_kernel,
        grid=(n_tiles,),
        in_specs=[_row_spec(d), _row_spec(HG_WIDTH), _row_spec(FOX_WIDTH),
                  _row_spec(d), _row_spec(d),
                  _resident((HG_WIDTH, d)), _resident((FOX_WIDTH, d)), _resident((d, d))],
        out_specs=_row_spec(d),
        out_shape=jax.ShapeDtypeStruct((rows, d), F32),
        compiler_params=pltpu.CompilerParams(
            dimension_semantics=("parallel",), vmem_limit_bytes=40 * 1024 * 1024),
        name="mix",
    )(hp, o_hg, o_fox, ga, gb, w_branch_a[layer].astype(BF16),
      w_branch_b[layer].astype(BF16), w_out[layer].astype(BF16))

    h2 = pl.pallas_call(
        _ffn_kernel,
        grid=(n_tiles,),
        in_specs=[_row_spec(d), _resident((1, d)), _resident((d, 2 * D_FF)),
                  _resident((CONV_WIDTH, 2 * D_FF)), _resident((1, 2 * D_FF)),
                  _resident((D_FF, d))],
        out_specs=_row_spec(d),
        out_shape=jax.ShapeDtypeStruct((rows, d), F32),
        scratch_shapes=[pltpu.VMEM((2 * SUBLANES, 2 * D_FF), F32),
                        pltpu.VMEM((d // LANES, ROW_TILE, LANES), F32)],
        compiler_params=pltpu.CompilerParams(
            dimension_semantics=("arbitrary",), vmem_limit_bytes=52 * 1024 * 1024),
        name="ffn",
    )(h1, norm2_gain[layer].astype(F32).reshape(1, d), w_up[layer].astype(BF16),
      conv_w[layer].astype(F32), conv_b[layer].astype(F32).reshape(1, 2 * D_FF),
      w_down[layer].astype(BF16))

    return h2.reshape(batch, lp, d)[:, LEAD:, :].astype(x.dtype)
```

```python
import functools

import jax
import jax.numpy as jnp
from jax import lax
from jax.experimental import pallas as pl
from jax.experimental.pallas import tpu as pltpu

F32 = jnp.float32
BF16 = jnp.bfloat16

D_MODEL = 1024
N_META = 16
FOX_HEADS = 8
FOX_HEAD_DIM = 64
FOX_WIDTH = FOX_HEADS * FOX_HEAD_DIM
HG_HEADS = 4
HG_DIM = 128
HG_WIDTH = HG_HEADS * HG_DIM
D_FF = 2816
CONV_WIDTH = 3
EPS = 1e-6

LANES = 128
SUBLANES = 8
LEAD = 128
PAD_ROWS = LEAD - N_META
ROW_TILE = 384
GROUP_LANES = 256
HEADS_PER_GROUP = GROUP_LANES // FOX_HEAD_DIM
ATT_BLOCK = 1024
KEY_BLOCK = 256
KEY_UNROLL = 4
LOG2E = 1.4426950408889634
HG_CHUNK = 128
HG_BASE = 16
HG_GROUP = 2
HG_UNROLL = 3
FF_CHUNK = 256
MASK_VALUE = -1e30

C_FQ, C_FK, C_FV, C_HQ, C_HF, C_HI, C_HG, C_GA, C_GB, C_FF, C_END = (
    0, 512, 1024, 1536, 2048, 2560, 3072, 3584, 4608, 5632, 5760)

NT_DIMS = (((1,), (1,)), ((), ()))


def _sigmoid(x):
    return 1.0 / (1.0 + jnp.exp(-x))


def _rms_rows(x, gain):
    ms = jnp.mean(x * x, axis=-1, keepdims=True)
    return x * lax.rsqrt(ms + EPS) * gain


def _padded_tile(x_ref, meta_ref, tile_in_seq):
    xb = x_ref[...]
    tm, d = xb.shape
    lead = jnp.concatenate([jnp.zeros((PAD_ROWS, d), F32), meta_ref[...]], axis=0)
    first = jnp.concatenate([lead, xb[:tm - LEAD, :]], axis=0)
    return jnp.where(tile_in_seq == 0, first, xb)


def _frame_window_spec(tiles_per_seq, tm, d):
    assert tm % LEAD == 0
    return pl.BlockSpec(
        (None, pl.Element(tm), pl.Element(d)),
        lambda i: (i // tiles_per_seq,
                   jnp.maximum((i % tiles_per_seq) * (tm // LEAD) - 1, 0) * LEAD, 0))


def _resident(shape):
    nd = len(shape)
    return pl.BlockSpec(shape, lambda *_: (0,) * nd, pipeline_mode=pl.Buffered(1))


def _proj_kernel(x_ref, meta_ref, g1_ref, w_ref, bf_ref, gq_ref, gk_ref, lb_ref, gm_ref,
                 q_ref, k_ref, v_ref, cum_ref, hq_ref, kin_ref, hi_ref, hgs_ref,
                 glog_ref, ga_ref, gb_ref, carry_ref, *, tiles_per_seq):
    tm = x_ref.shape[0]
    tile_in_seq = pl.program_id(0) % tiles_per_seq

    @pl.when(tile_in_seq == 0)
    def _():
        carry_ref[...] = jnp.zeros_like(carry_ref)

    xn = _rms_rows(_padded_tile(x_ref, meta_ref, tile_in_seq), g1_ref[...]).astype(BF16)

    def proj(a, b):
        return jnp.dot(xn, w_ref[:, a:b], preferred_element_type=F32)

    gm = gm_ref[...]

    def head_norm(y, gain):
        msq = jnp.dot((y * y).astype(BF16), gm, preferred_element_type=F32)
        return y * lax.rsqrt(msq + EPS) * gain

    q_ref[...] = head_norm(proj(C_FQ, C_FK), gq_ref[...]).astype(BF16)
    k_ref[...] = head_norm(proj(C_FK, C_FV), gk_ref[...]).astype(BF16)
    v_ref[...] = proj(C_FV, C_HQ).astype(BF16)

    z = proj(C_FF, C_END) + bf_ref[...]
    logf = -(jnp.maximum(-z, 0.0) + jnp.log(1.0 + jnp.exp(-jnp.abs(z))))
    p0 = logf.astype(BF16)
    r0 = logf - p0.astype(F32)
    p1 = r0.astype(BF16)
    p2 = (r0 - p1.astype(F32)).astype(BF16)
    row = lax.broadcasted_iota(jnp.int32, (tm, tm), 0)
    col = lax.broadcasted_iota(jnp.int32, (tm, tm), 1)
    tri = jnp.where(col <= row, 1.0, 0.0).astype(BF16)
    cum = (jnp.dot(tri, p0, preferred_element_type=F32)
           + jnp.dot(tri, p1, preferred_element_type=F32)
           + jnp.dot(tri, p2, preferred_element_type=F32)
           + carry_ref[...])
    cum_ref[...] = cum[:, :FOX_HEADS]
    carry_ref[...] = cum[tm - 1:tm, :]

    hq_ref[...] = proj(C_HQ, C_HF).astype(BF16)
    hf = proj(C_HF, C_HI)
    t = jnp.exp(-jnp.abs(hf))
    r = 1.0 / (1.0 + t)
    tr = t * r
    pos = hf >= 0.0
    sig_pos = jnp.where(pos, r, tr)
    sig_neg = jnp.where(pos, tr, r)
    lb = lb_ref[...]
    glog_ref[...] = jnp.log(lb + (1.0 - lb) * sig_pos)
    kin_ref[...] = ((1.0 - lb) * sig_neg).astype(BF16)
    hi_ref[...] = proj(C_HI, C_HG).astype(BF16)
    hg = proj(C_HG, C_GA)
    hgs_ref[...] = (hg * _sigmoid(hg)).astype(BF16)

    ga_ref[...] = _sigmoid(proj(C_GA, C_GB)).astype(BF16)
    gb_ref[...] = _sigmoid(proj(C_GB, C_FF)).astype(BF16)


def _attn_kernel(q_ref, k_ref, v_ref, crm_ref, crr_ref, o_ref,
                 qm_ref, acc_ref, m_ref, *, n_real):
    hp = HEADS_PER_GROUP
    lane_head = lax.broadcasted_iota(jnp.int32, (1, GROUP_LANES), 1) // FOX_HEAD_DIM

    def setup(qs, tq):
        qb = q_ref[pl.ds(qs, tq), :]
        for j in range(hp):
            qm_ref[j, :tq, :] = jnp.where(lane_head == j, qb, jnp.zeros_like(qb))
            m_ref[j, :tq, :] = jnp.full((tq, LANES), MASK_VALUE, F32)
            acc_ref[j, :tq, :] = jnp.zeros((tq, GROUP_LANES), F32)

    def block(r0, nr, ks, tk, c0, cr, mask):
        kb = k_ref[pl.ds(ks, tk), :]
        vb = v_ref[pl.ds(ks, tk), :]
        bias = (c0 - cr) * LOG2E
        rows = slice(r0, r0 + nr)
        for j in range(hp):
            s = lax.dot_general(qm_ref[j, rows, :], kb, NT_DIMS,
                                preferred_element_type=F32)
            s = s + bias[j:j + 1, :]
            if mask is not None:
                s = jnp.where(mask, s, MASK_VALUE)
            m_old = m_ref[j, rows, :]
            m_new = jnp.maximum(m_old, jnp.max(s, axis=-1, keepdims=True))
            alpha = jnp.exp2(m_old - m_new)
            p = jnp.exp2(s - jnp.concatenate([m_new] * (tk // LANES), axis=1))
            vj = jnp.where(lane_head == j, vb,
                           jnp.where(lane_head == (j + 1) % hp, 1.0, 0.0).astype(BF16))
            acc_ref[j, rows, :] = (
                jnp.concatenate([alpha] * (GROUP_LANES // LANES), axis=1) * acc_ref[j, rows, :]
                + jnp.dot(p.astype(BF16), vj, preferred_element_type=F32))
            m_ref[j, rows, :] = m_new

    def finalize(qs, tq):
        out = jnp.zeros((tq, GROUP_LANES), F32)
        for j in range(hp):
            acc = acc_ref[j, :tq, :]
            c = ((j + 1) % hp) * FOX_HEAD_DIM
            out = jnp.where(lane_head == j, acc * (1.0 / acc[:, c:c + 1]), out)
        o_ref[pl.ds(qs, tq), :] = out.astype(BF16)

    crm = crm_ref[...]
    row_m = lax.broadcasted_iota(jnp.int32, (LEAD, LEAD), 0)
    col_m = lax.broadcasted_iota(jnp.int32, (LEAD, LEAD), 1)
    setup(0, LEAD)
    block(0, LEAD, 0, LEAD, crm[:, PAD_ROWS:PAD_ROWS + 1], crm,
          (col_m >= PAD_ROWS) & (col_m <= row_m))
    finalize(0, LEAD)

    tq, tk = ATT_BLOCK, KEY_BLOCK
    kpq = tq // tk
    lead_mask = lax.broadcasted_iota(jnp.int32, (tq, LEAD), 1) >= PAD_ROWS

    def q_step(i, carry):
        qs = pl.multiple_of(LEAD + i * tq, LANES)
        c0 = crr_ref[i * kpq][:, 0:1]
        setup(qs, tq)
        block(0, tq, 0, LEAD, c0, crm, lead_mask)

        def k_step(jq, c):
            for d in range(KEY_UNROLL):
                jb = jq * KEY_UNROLL + d
                ks = pl.multiple_of(LEAD + jb * tk, LANES)
                block(0, tq, ks, tk, c0, crr_ref[jb], None)
            return c

        lax.fori_loop(0, i * (kpq // KEY_UNROLL), k_step, 0)
        for d in range(kpq):
            causal = (lax.broadcasted_iota(jnp.int32, (tq, tk), 1) + d * tk
                      <= lax.broadcasted_iota(jnp.int32, (tq, tk), 0))
            block(0, tq, pl.multiple_of(qs + d * tk, LANES), tk, c0,
                  crr_ref[i * kpq + d], causal)
        finalize(qs, tq)
        return carry

    lax.fori_loop(0, n_real, q_step, 0)


def _hgrn_kernel(hq_ref, kin_ref, hi_ref, hgs_ref, gl_ref, gain_ref, o_ref,
                 st_ref, *, n_chunk):
    c_len = HG_CHUNK
    st_ref[...] = jnp.zeros_like(st_ref)
    row = lax.broadcasted_iota(jnp.int32, (c_len, c_len), 0)
    col = lax.broadcasted_iota(jnp.int32, (c_len, c_len), 1)
    rowv = lax.broadcasted_iota(jnp.int32, (c_len, HG_DIM), 0)
    gain = gain_ref[...]

    def ref_rows(g, period, offset):
        parts = [jnp.broadcast_to(g[a * period + offset:a * period + offset + 1, :],
                                  (period, HG_DIM))
                 for a in range(c_len // period)]
        return jnp.concatenate(parts, axis=0)

    def chunk(c, hh):
        rows = pl.ds(pl.multiple_of(c * c_len, c_len), c_len)
        cols = slice(hh * HG_DIM, (hh + 1) * HG_DIM)
        g = gl_ref[rows, cols]
        d = 1
        while d < c_len:
            g = g + jnp.where(rowv >= d, pltpu.roll(g, d, 0), 0.0)
            d *= 2
        hq = hq_ref[rows, cols].astype(F32)
        kin = kin_ref[rows, cols].astype(F32)
        v = hi_ref[rows, cols]

        a_mat = jnp.zeros((c_len, c_len), F32)
        bs = c_len // 2
        while bs >= HG_BASE:
            ref = ref_rows(g, 2 * bs, bs - 1)
            ql = (hq * jnp.exp(jnp.minimum(g - ref, 0.0))).astype(BF16)
            kl = (kin * jnp.exp(jnp.minimum(ref - g, 0.0))).astype(BF16)
            al = lax.dot_general(ql, kl, NT_DIMS, preferred_element_type=F32)
            same = (row // (2 * bs)) == (col // (2 * bs))
            ml = same & ((row & bs) != 0) & ((col & bs) == 0)
            a_mat = jnp.where(ml, al, a_mat)
            bs //= 2
        ref = ref_rows(g, HG_BASE, HG_BASE // 2 - 1)
        qd = (hq * jnp.exp(g - ref)).astype(BF16)
        kd = (kin * jnp.exp(ref - g)).astype(BF16)
        ad = lax.dot_general(qd, kd, NT_DIMS, preferred_element_type=F32)
        md = ((row // HG_BASE) == (col // HG_BASE)) & (col <= row)
        a_mat = jnp.where(md, ad, a_mat)

        st = st_ref[hh]
        o = jnp.dot(a_mat.astype(BF16), v, preferred_element_type=F32)
        o = o + lax.dot_general((hq * jnp.exp(g)).astype(BF16), st.astype(BF16),
                                NT_DIMS, preferred_element_type=F32)

        g_tot = g[c_len - 1:c_len, :]
        kt = (kin * jnp.exp(g_tot - g)).astype(BF16)
        v_t = v.astype(F32).T.astype(BF16)
        st_ref[hh] = st * jnp.exp(g_tot) + jnp.dot(v_t, kt, preferred_element_type=F32)

        y = _rms_rows(o, gain) * hgs_ref[rows, cols].astype(F32)
        o_ref[rows, cols] = y.astype(BF16)

    def body(cu, carry):
        for u in range(HG_UNROLL):
            for hh in range(HG_GROUP):
                chunk(cu * HG_UNROLL + u, hh)
        return carry

    lax.fori_loop(0, n_chunk // HG_UNROLL, body, 0)


def _mix_kernel(x_ref, meta_ref, ohg_ref, ofox_ref, ga_ref, gb_ref, wa_ref, wb_ref, wo_ref,
                o_ref, *, tiles_per_seq):
    h = _padded_tile(x_ref, meta_ref, pl.program_id(0) % tiles_per_seq)
    y_a = jnp.dot(ohg_ref[...], wa_ref[...], preferred_element_type=F32)
    y_b = jnp.dot(ofox_ref[...], wb_ref[...], preferred_element_type=F32)
    merged = ga_ref[...].astype(F32) * y_a + gb_ref[...].astype(F32) * y_b
    o_ref[...] = h + jnp.dot(merged.astype(BF16), wo_ref[...], preferred_element_type=F32)


def _ffn_kernel(h_ref, g2_ref, wup_ref, cw_ref, cb_ref, wdn_ref, o_ref, carry_ref, perm_ref):
    tm = h_ref.shape[0]
    nv = tm // SUBLANES

    @pl.when(pl.program_id(0) == 0)
    def _():
        carry_ref[...] = jnp.zeros_like(carry_ref)

    n_slab = h_ref.shape[1] // LANES
    for j in range(n_slab):
        perm_ref[j] = h_ref[:, j * LANES:(j + 1) * LANES]
    h = jnp.concatenate(
        [jnp.concatenate([perm_ref[j, pl.ds(k, SUBLANES, stride=nv), :] for k in range(nv)],
                         axis=0) for j in range(n_slab)], axis=1)
    hn = _rms_rows(h, g2_ref[...]).astype(BF16)
    sub = lax.broadcasted_iota(jnp.int32, (SUBLANES, FF_CHUNK), 0)

    def up_cols(lo):
        return jnp.dot(hn, wup_ref[:, lo:lo + FF_CHUNK], preferred_element_type=F32)

    def conv_cols(u, lo):
        cols = slice(lo, lo + FF_CHUNK)
        prev = carry_ref[:, cols]
        carry_ref[:, cols] = u[tm - 2 * SUBLANES:tm, :]

        def wrap(prev8, last8):
            return jnp.where(sub == 0, pltpu.roll(prev8, 1, 0), pltpu.roll(last8, 1, 0))

        f1 = wrap(prev[SUBLANES:, :], u[tm - SUBLANES:tm, :])
        f2 = wrap(prev[:SUBLANES, :], u[tm - 2 * SUBLANES:tm - SUBLANES, :])
        s1 = jnp.concatenate([f1, u[:tm - SUBLANES, :]], axis=0)
        s2 = jnp.concatenate([f2, f1, u[:tm - 2 * SUBLANES, :]], axis=0)
        cw = cw_ref[:, cols]
        return cw[0:1, :] * s2 + cw[1:2, :] * s1 + cw[2:3, :] * u + cb_ref[:, cols]

    n_chunks = D_FF // FF_CHUNK
    nxt = (up_cols(0), up_cols(D_FF))
    acc = h
    for c in range(n_chunks):
        u_g, u_v = nxt
        if c + 1 < n_chunks:
            nxt = (up_cols((c + 1) * FF_CHUNK), up_cols(D_FF + (c + 1) * FF_CHUNK))
        u_gate = conv_cols(u_g, c * FF_CHUNK)
        u_val = conv_cols(u_v, D_FF + c * FF_CHUNK)
        act = (u_gate * _sigmoid(u_gate) * u_val).astype(BF16)
        acc = acc + jnp.dot(act, wdn_ref[c * FF_CHUNK:(c + 1) * FF_CHUNK, :],
                            preferred_element_type=F32)
    for j in range(n_slab):
        for k in range(nv):
            perm_ref[j, pl.ds(k, SUBLANES, stride=nv), :] = acc[
                k * SUBLANES:(k + 1) * SUBLANES, j * LANES:(j + 1) * LANES]
        o_ref[:, j * LANES:(j + 1) * LANES] = perm_ref[j]


def _row_spec(width, tm=ROW_TILE):
    return pl.BlockSpec((tm, width), lambda i: (i, 0))


def kernel(x, meta_tokens, norm1_gain, w_in, fox_b_f, q_norm_gain, k_norm_gain,
           hg_lb_logits, hg_out_gain, w_branch_a, w_branch_b, w_out, norm2_gain,
           w_up, conv_w, conv_b, w_down):
    batch, seq, d = x.shape
    lp = LEAD + seq
    rows = batch * lp
    assert d == D_MODEL and lp % ROW_TILE == 0 and seq % ATT_BLOCK == 0
    assert lp % (HG_CHUNK * HG_UNROLL) == 0 and D_FF % FF_CHUNK == 0
    n_tiles = rows // ROW_TILE
    layer = 0

    xf = x.astype(F32)
    meta = meta_tokens.astype(F32)
    tiles_per_seq = lp // ROW_TILE

    w = w_in[layer]
    o_fq, o_fk, o_fv, o_ff = 0, 512, 1024, 1536
    o_hq = o_ff + FOX_HEADS
    o_hf, o_hi, o_hg = o_hq + 512, o_hq + 1024, o_hq + 1536
    o_ga = o_hq + 2048
    o_gb = o_ga + D_MODEL
    w1 = jnp.concatenate([
        w[:, o_fq:o_fk], w[:, o_fk:o_fv], w[:, o_fv:o_ff],
        w[:, o_hq:o_hf], w[:, o_hf:o_hi], w[:, o_hi:o_hg], w[:, o_hg:o_ga],
        w[:, o_ga:o_gb], w[:, o_gb:o_gb + D_MODEL],
        w[:, o_ff:o_hq], jnp.zeros((d, LANES - FOX_HEADS), w.dtype)], axis=1).astype(BF16)
    bf = jnp.concatenate([fox_b_f[layer].astype(F32),
                          jnp.zeros((LANES - FOX_HEADS,), F32)]).reshape(1, LANES)
    scale = FOX_HEAD_DIM ** -0.5 * LOG2E
    gq = (jnp.tile(q_norm_gain[layer].astype(F32), FOX_HEADS) * scale).reshape(1, FOX_WIDTH)
    gk = jnp.tile(k_norm_gain[layer].astype(F32), FOX_HEADS).reshape(1, FOX_WIDTH)
    lower_bounds = jnp.cumsum(jax.nn.softmax(hg_lb_logits.astype(F32), axis=0), axis=0)
    lb = lower_bounds[layer].reshape(1, HG_WIDTH)
    head_id = jnp.arange(FOX_WIDTH) // FOX_HEAD_DIM
    gm = jnp.where(head_id[:, None] == head_id[None, :], 1.0 / FOX_HEAD_DIM, 0.0).astype(BF16)

    bf16_w = lambda width: jax.ShapeDtypeStruct((rows, width), BF16)
    outs = pl.pallas_call(
        functools.partial(_proj_kernel, tiles_per_seq=tiles_per_seq),
        grid=(n_tiles,),
        in_specs=[_frame_window_spec(tiles_per_seq, ROW_TILE, d), _resident((N_META, d)),
                  _resident((1, d)), _resident((d, C_END)),
                  _resident((1, LANES)), _resident((1, FOX_WIDTH)), _resident((1, FOX_WIDTH)),
                  _resident((1, HG_WIDTH)), _resident((FOX_WIDTH, FOX_WIDTH))],
        out_specs=[_row_spec(FOX_WIDTH), _row_spec(FOX_WIDTH), _row_spec(FOX_WIDTH),
                   _row_spec(FOX_HEADS),
                   _row_spec(HG_WIDTH), _row_spec(HG_WIDTH), _row_spec(HG_WIDTH),
                   _row_spec(HG_WIDTH), _row_spec(HG_WIDTH),
                   _row_spec(d), _row_spec(d)],
        out_shape=[bf16_w(FOX_WIDTH), bf16_w(FOX_WIDTH), bf16_w(FOX_WIDTH),
                   jax.ShapeDtypeStruct((rows, FOX_HEADS), F32),
                   bf16_w(HG_WIDTH), bf16_w(HG_WIDTH), bf16_w(HG_WIDTH), bf16_w(HG_WIDTH),
                   jax.ShapeDtypeStruct((rows, HG_WIDTH), F32),
                   bf16_w(d), bf16_w(d)],
        scratch_shapes=[pltpu.VMEM((1, LANES), F32)],
        compiler_params=pltpu.CompilerParams(
            dimension_semantics=("arbitrary",), vmem_limit_bytes=52 * 1024 * 1024),
        name="proj",
    )(xf, meta, norm1_gain[layer].astype(F32).reshape(1, d), w1, bf, gq, gk, lb, gm)
    fq, fk, fv, cum, hq, kin, hi, hgs, glog, ga, gb = outs

    n_grp = FOX_HEADS // HEADS_PER_GROUP
    n_real = seq // ATT_BLOCK
    cum_row = jnp.transpose(cum.reshape(batch, lp, n_grp, HEADS_PER_GROUP), (0, 2, 3, 1))
    cum_row_lead = cum_row[..., :LEAD]
    n_kb = seq // KEY_BLOCK
    cum_row_real = jnp.transpose(
        cum_row[..., LEAD:].reshape(batch, n_grp, HEADS_PER_GROUP, n_kb, KEY_BLOCK),
        (0, 1, 3, 2, 4))
    seq_spec = pl.BlockSpec((lp, GROUP_LANES), lambda b, g: (b, g))
    o_fox = pl.pallas_call(
        functools.partial(_attn_kernel, n_real=n_real),
        grid=(batch, n_grp),
        in_specs=[seq_spec, seq_spec, seq_spec,
                  pl.BlockSpec((None, None, HEADS_PER_GROUP, LEAD), lambda b, g: (b, g, 0, 0)),
                  pl.BlockSpec((None, None, n_kb, HEADS_PER_GROUP, KEY_BLOCK),
                               lambda b, g: (b, g, 0, 0, 0))],
        out_specs=seq_spec,
        out_shape=jax.ShapeDtypeStruct((rows, FOX_WIDTH), BF16),
        scratch_shapes=[pltpu.VMEM((HEADS_PER_GROUP, ATT_BLOCK, GROUP_LANES), BF16),
                        pltpu.VMEM((HEADS_PER_GROUP, ATT_BLOCK, GROUP_LANES), F32),
                        pltpu.VMEM((HEADS_PER_GROUP, ATT_BLOCK, LANES), F32)],
        compiler_params=pltpu.CompilerParams(
            dimension_semantics=("parallel", "parallel"), vmem_limit_bytes=48 * 1024 * 1024),
        name="fox_attn",
    )(fq, fk, fv, cum_row_lead, cum_row_real)

    head_spec = pl.BlockSpec((lp, HG_GROUP * HG_DIM), lambda b, hh: (b, hh))
    o_hg = pl.pallas_call(
        functools.partial(_hgrn_kernel, n_chunk=lp // HG_CHUNK),
        grid=(batch, HG_HEADS // HG_GROUP),
        in_specs=[head_spec, head_spec, head_spec, head_spec, head_spec,
                  pl.BlockSpec((1, HG_DIM), lambda b, hh: (0, 0))],
        out_specs=head_spec,
        out_shape=jax.ShapeDtypeStruct((rows, HG_WIDTH), BF16),
        scratch_shapes=[pltpu.VMEM((HG_GROUP, HG_DIM, HG_DIM), F32)],
        compiler_params=pltpu.CompilerParams(
            dimension_semantics=("parallel", "parallel"), vmem_limit_bytes=40 * 1024 * 1024),
        name="hgrn2",
    )(hq, kin, hi, hgs, glog, hg_out_gain[layer].astype(F32).reshape(1, HG_DIM))

    h1 = pl.pallas_call(
        functools.partial(_mix_kernel, tiles_per_seq=tiles_per_seq),
        grid=(n_tiles,),
        in_specs=[_frame_window_spec(tiles_per_seq, ROW_TILE, d), _resident((N_META, d)),
                  _row_spec(HG_WIDTH), _row_spec(FOX_WIDTH),
                  _row_spec(d), _row_spec(d),
                  _resident((HG_WIDTH, d)), _resident((FOX_WIDTH, d)), _resident((d, d))],
        out_specs=_row_spec(d),
        out_shape=jax.ShapeDtypeStruct((rows, d), F32),
        compiler_params=pltpu.CompilerParams(
            dimension_semantics=("parallel",), vmem_limit_bytes=40 * 1024 * 1024),
        name="mix",
    )(xf, meta, o_hg, o_fox, ga, gb, w_branch_a[layer].astype(BF16),
      w_branch_b[layer].astype(BF16), w_out[layer].astype(BF16))

    h2 = pl.pallas_call(
        _ffn_kernel,
        grid=(n_tiles,),
        in_specs=[_row_spec(d), _resident((1, d)), _resident((d, 2 * D_FF)),
                  _resident((CONV_WIDTH, 2 * D_FF)), _resident((1, 2 * D_FF)),
                  _resident((D_FF, d))],
        out_specs=_row_spec(d),
        out_shape=jax.ShapeDtypeStruct((rows, d), F32),
        scratch_shapes=[pltpu.VMEM((2 * SUBLANES, 2 * D_FF), F32),
                        pltpu.VMEM((d // LANES, ROW_TILE, LANES), F32)],
        compiler_params=pltpu.CompilerParams(
            dimension_semantics=("arbitrary",), vmem_limit_bytes=52 * 1024 * 1024),
        name="ffn",
    )(h1, norm2_gain[layer].astype(F32).reshape(1, d), w_up[layer].astype(BF16),
      conv_w[layer].astype(F32), conv_b[layer].astype(F32).reshape(1, 2 * D_FF),
      w_down[layer].astype(BF16))

    return h2.reshape(batch, lp, d)[:, LEAD:, :].astype(x.dtype)
```

```python
import functools

import jax
import jax.numpy as jnp
from jax import lax
from jax.experimental import pallas as pl
from jax.experimental.pallas import tpu as pltpu

F32 = jnp.float32
BF16 = jnp.bfloat16

D_MODEL = 1024
N_META = 16
FOX_HEADS = 8
FOX_HEAD_DIM = 64
FOX_WIDTH = FOX_HEADS * FOX_HEAD_DIM
HG_HEADS = 4
HG_DIM = 128
HG_WIDTH = HG_HEADS * HG_DIM
D_FF = 2816
CONV_WIDTH = 3
EPS = 1e-6

LANES = 128
SUBLANES = 8
LEAD = 128
PAD_ROWS = LEAD - N_META
ROW_TILE = 384
GROUP_LANES = 256
HEADS_PER_GROUP = GROUP_LANES // FOX_HEAD_DIM
ATT_BLOCK = 1024
KEY_BLOCK = 256
KEY_UNROLL = 4
LOG2E = 1.4426950408889634
HG_CHUNK = 128
HG_BASE = 16
HG_GROUP = 2
HG_UNROLL = 3
FF_CHUNK = 256
MASK_VALUE = -1e30

C_FQ, C_FK, C_FV, C_HQ, C_HF, C_HI, C_HG, C_GA, C_GB, C_FF, C_END = (
    0, 512, 1024, 1536, 2048, 2560, 3072, 3584, 4608, 5632, 5760)

NT_DIMS = (((1,), (1,)), ((), ()))


def _sigmoid(x):
    return 1.0 / (1.0 + jnp.exp(-x))


def _rms_rows(x, gain):
    ms = jnp.mean(x * x, axis=-1, keepdims=True)
    return x * lax.rsqrt(ms + EPS) * gain


def _padded_tile(x_ref, meta_ref, tile_in_seq):
    xb = x_ref[...]
    tm, d = xb.shape
    lead = jnp.concatenate([jnp.zeros((PAD_ROWS, d), F32), meta_ref[...]], axis=0)
    first = jnp.concatenate([lead, xb[:tm - LEAD, :]], axis=0)
    return jnp.where(tile_in_seq == 0, first, xb)


def _clamp(i, last):
    return i if last is None else jnp.minimum(i, last)


def _frame_window_spec(tiles_per_seq, tm, d, last=None):
    assert tm % LEAD == 0

    def index_map(i):
        i = _clamp(i, last)
        return (i // tiles_per_seq,
                jnp.maximum((i % tiles_per_seq) * (tm // LEAD) - 1, 0) * LEAD, 0)

    return pl.BlockSpec((None, pl.Element(tm), pl.Element(d)), index_map)


def _resident(shape):
    nd = len(shape)
    return pl.BlockSpec(shape, lambda *_: (0,) * nd, pipeline_mode=pl.Buffered(1))


def _proj_kernel(x_ref, meta_ref, g1_ref, w_ref, bf_ref, gq_ref, gk_ref, lb_ref, gm_ref,
                 q_ref, k_ref, v_ref, cum_ref, hq_ref, kin_ref, hi_ref, hgs_ref,
                 glog_ref, ga_ref, gb_ref, carry_ref, *, tiles_per_seq):
    tm = x_ref.shape[0]
    tile_in_seq = pl.program_id(0) % tiles_per_seq

    @pl.when(tile_in_seq == 0)
    def _():
        carry_ref[...] = jnp.zeros_like(carry_ref)

    xn = _rms_rows(_padded_tile(x_ref, meta_ref, tile_in_seq), g1_ref[...]).astype(BF16)

    def proj(a, b):
        return jnp.dot(xn, w_ref[:, a:b], preferred_element_type=F32)

    gm = gm_ref[...]

    def head_norm(y, gain):
        msq = jnp.dot((y * y).astype(BF16), gm, preferred_element_type=F32)
        return y * lax.rsqrt(msq + EPS) * gain

    q_ref[...] = head_norm(proj(C_FQ, C_FK), gq_ref[...]).astype(BF16)
    k_ref[...] = head_norm(proj(C_FK, C_FV), gk_ref[...]).astype(BF16)
    v_ref[...] = proj(C_FV, C_HQ).astype(BF16)

    z = proj(C_FF, C_END) + bf_ref[...]
    logf = -(jnp.maximum(-z, 0.0) + jnp.log(1.0 + jnp.exp(-jnp.abs(z))))
    rowi = lax.broadcasted_iota(jnp.int32, (tm, LANES), 0)
    cum = logf
    step = 1
    while step < tm:
        cum = cum + jnp.where(rowi >= step, pltpu.roll(cum, step, 0), 0.0)
        step *= 2
    cum = cum + carry_ref[...]
    cum_ref[...] = cum[:, :FOX_HEADS]
    carry_ref[...] = cum[tm - 1:tm, :]

    hq_ref[...] = proj(C_HQ, C_HF).astype(BF16)
    hf = proj(C_HF, C_HI)
    t = jnp.exp(-jnp.abs(hf))
    r = 1.0 / (1.0 + t)
    tr = t * r
    pos = hf >= 0.0
    sig_pos = jnp.where(pos, r, tr)
    sig_neg = jnp.where(pos, tr, r)
    lb = lb_ref[...]
    glog_ref[...] = jnp.log(lb + (1.0 - lb) * sig_pos)
    kin_ref[...] = ((1.0 - lb) * sig_neg).astype(BF16)
    hi_ref[...] = proj(C_HI, C_HG).astype(BF16)
    hg = proj(C_HG, C_GA)
    hgs_ref[...] = (hg * _sigmoid(hg)).astype(BF16)

    ga_ref[...] = _sigmoid(proj(C_GA, C_GB)).astype(BF16)
    gb_ref[...] = _sigmoid(proj(C_GB, C_FF)).astype(BF16)


def _attn_kernel(q_ref, k_ref, v_ref, crm_ref, crr_ref, o_ref,
                 qm_ref, acc_ref, m_ref, *, n_real):
    hp = HEADS_PER_GROUP
    lane_head = lax.broadcasted_iota(jnp.int32, (1, GROUP_LANES), 1) // FOX_HEAD_DIM

    def setup(qs, tq):
        qb = q_ref[pl.ds(qs, tq), :]
        for j in range(hp):
            qm_ref[j, :tq, :] = jnp.where(lane_head == j, qb, jnp.zeros_like(qb))
            m_ref[j, :tq, :] = jnp.full((tq, LANES), MASK_VALUE, F32)
            acc_ref[j, :tq, :] = jnp.zeros((tq, GROUP_LANES), F32)

    def block(r0, nr, ks, tk, c0, cr, mask):
        kb = k_ref[pl.ds(ks, tk), :]
        vb = v_ref[pl.ds(ks, tk), :]
        bias = (c0 - cr) * LOG2E
        rows = slice(r0, r0 + nr)
        for j in range(hp):
            s = lax.dot_general(qm_ref[j, rows, :], kb, NT_DIMS,
                                preferred_element_type=F32)
            s = s + bias[j:j + 1, :]
            if mask is not None:
                s = jnp.where(mask, s, MASK_VALUE)
            m_old = m_ref[j, rows, :]
            m_new = jnp.maximum(m_old, jnp.max(s, axis=-1, keepdims=True))
            alpha = jnp.exp2(m_old - m_new)
            p = jnp.exp2(s - jnp.concatenate([m_new] * (tk // LANES), axis=1))
            vj = jnp.where(lane_head == j, vb,
                           jnp.where(lane_head == (j + 1) % hp, 1.0, 0.0).astype(BF16))
            acc_ref[j, rows, :] = (
                jnp.concatenate([alpha] * (GROUP_LANES // LANES), axis=1) * acc_ref[j, rows, :]
                + jnp.dot(p.astype(BF16), vj, preferred_element_type=F32))
            m_ref[j, rows, :] = m_new

    def finalize(qs, tq):
        out = jnp.zeros((tq, GROUP_LANES), F32)
        for j in range(hp):
            acc = acc_ref[j, :tq, :]
            c = ((j + 1) % hp) * FOX_HEAD_DIM
            out = jnp.where(lane_head == j, acc * (1.0 / acc[:, c:c + 1]), out)
        o_ref[pl.ds(qs, tq), :] = out.astype(BF16)

    crm = crm_ref[...]
    row_m = lax.broadcasted_iota(jnp.int32, (LEAD, LEAD), 0)
    col_m = lax.broadcasted_iota(jnp.int32, (LEAD, LEAD), 1)
    setup(0, LEAD)
    block(0, LEAD, 0, LEAD, crm[:, PAD_ROWS:PAD_ROWS + 1], crm,
          (col_m >= PAD_ROWS) & (col_m <= row_m))
    finalize(0, LEAD)

    tq, tk = ATT_BLOCK, KEY_BLOCK
    kpq = tq // tk
    lead_mask = lax.broadcasted_iota(jnp.int32, (tq, LEAD), 1) >= PAD_ROWS

    def q_step(i, carry):
        qs = pl.multiple_of(LEAD + i * tq, LANES)
        c0 = crr_ref[i * kpq][:, 0:1]
        setup(qs, tq)
        block(0, tq, 0, LEAD, c0, crm, lead_mask)

        def k_step(jq, c):
            for d in range(KEY_UNROLL):
                jb = jq * KEY_UNROLL + d
                ks = pl.multiple_of(LEAD + jb * tk, LANES)
                block(0, tq, ks, tk, c0, crr_ref[jb], None)
            return c

        lax.fori_loop(0, i * (kpq // KEY_UNROLL), k_step, 0)
        for d in range(kpq):
            causal = (lax.broadcasted_iota(jnp.int32, (tq, tk), 1) + d * tk
                      <= lax.broadcasted_iota(jnp.int32, (tq, tk), 0))
            block(0, tq, pl.multiple_of(qs + d * tk, LANES), tk, c0,
                  crr_ref[i * kpq + d], causal)
        finalize(qs, tq)
        return carry

    lax.fori_loop(0, n_real, q_step, 0)


def _hgrn_kernel(hq_ref, kin_ref, hi_ref, hgs_ref, gl_ref, gain_ref, o_ref,
                 st_ref, *, n_chunk):
    c_len = HG_CHUNK
    st_ref[...] = jnp.zeros_like(st_ref)
    row = lax.broadcasted_iota(jnp.int32, (c_len, c_len), 0)
    col = lax.broadcasted_iota(jnp.int32, (c_len, c_len), 1)
    rowv = lax.broadcasted_iota(jnp.int32, (c_len, HG_DIM), 0)
    gain = gain_ref[...]

    def ref_rows(g, period, offset):
        parts = [jnp.broadcast_to(g[a * period + offset:a * period + offset + 1, :],
                                  (period, HG_DIM))
                 for a in range(c_len // period)]
        return jnp.concatenate(parts, axis=0)

    def chunk(c, hh):
        rows = pl.ds(pl.multiple_of(c * c_len, c_len), c_len)
        cols = slice(hh * HG_DIM, (hh + 1) * HG_DIM)
        g = gl_ref[rows, cols]
        d = 1
        while d < c_len:
            g = g + jnp.where(rowv >= d, pltpu.roll(g, d, 0), 0.0)
            d *= 2
        hq = hq_ref[rows, cols].astype(F32)
        kin = kin_ref[rows, cols].astype(F32)
        v = hi_ref[rows, cols]

        a_mat = jnp.zeros((c_len, c_len), F32)
        bs = c_len // 2
        while bs >= HG_BASE:
            ref = ref_rows(g, 2 * bs, bs - 1)
            ql = (hq * jnp.exp(jnp.minimum(g - ref, 0.0))).astype(BF16)
            kl = (kin * jnp.exp(jnp.minimum(ref - g, 0.0))).astype(BF16)
            al = lax.dot_general(ql, kl, NT_DIMS, preferred_element_type=F32)
            same = (row // (2 * bs)) == (col // (2 * bs))
            ml = same & ((row & bs) != 0) & ((col & bs) == 0)
            a_mat = jnp.where(ml, al, a_mat)
            bs //= 2
        ref = ref_rows(g, HG_BASE, HG_BASE // 2 - 1)
        qd = (hq * jnp.exp(g - ref)).astype(BF16)
        kd = (kin * jnp.exp(ref - g)).astype(BF16)
        ad = lax.dot_general(qd, kd, NT_DIMS, preferred_element_type=F32)
        md = ((row // HG_BASE) == (col // HG_BASE)) & (col <= row)
        a_mat = jnp.where(md, ad, a_mat)

        st = st_ref[hh]
        o = jnp.dot(a_mat.astype(BF16), v, preferred_element_type=F32)
        o = o + lax.dot_general((hq * jnp.exp(g)).astype(BF16), st.astype(BF16),
                                NT_DIMS, preferred_element_type=F32)

        g_tot = g[c_len - 1:c_len, :]
        kt = (kin * jnp.exp(g_tot - g)).astype(BF16)
        v_t = v.astype(F32).T.astype(BF16)
        st_ref[hh] = st * jnp.exp(g_tot) + jnp.dot(v_t, kt, preferred_element_type=F32)

        y = _rms_rows(o, gain) * hgs_ref[rows, cols].astype(F32)
        o_ref[rows, cols] = y.astype(BF16)

    def body(cu, carry):
        for u in range(HG_UNROLL):
            for hh in range(HG_GROUP):
                chunk(cu * HG_UNROLL + u, hh)
        return carry

    lax.fori_loop(0, n_chunk // HG_UNROLL, body, 0)


def _mix_kernel(x_ref, meta_ref, ohg_ref, ofox_ref, ga_ref, gb_ref, wa_ref, wb_ref, wo_ref,
                o_ref, *, tiles_per_seq):
    h = _padded_tile(x_ref, meta_ref, pl.program_id(0) % tiles_per_seq)
    y_a = jnp.dot(ohg_ref[...], wa_ref[...], preferred_element_type=F32)
    y_b = jnp.dot(ofox_ref[...], wb_ref[...], preferred_element_type=F32)
    merged = ga_ref[...].astype(F32) * y_a + gb_ref[...].astype(F32) * y_b
    o_ref[...] = h + jnp.dot(merged.astype(BF16), wo_ref[...], preferred_element_type=F32)


def _ffn_kernel(h_ref, hlead_ref, g2_ref, wup_ref, cw_ref, cb_ref, wdn_ref, o_ref,
                carry_ref, perm_ref):
    tm = h_ref.shape[0]
    nv = tm // SUBLANES
    lead_cols = 2 * FF_CHUNK

    @pl.when((pl.program_id(0) == 0) & (pl.program_id(1) == 0))
    def _():
        hl = _rms_rows(hlead_ref[...], g2_ref[...]).astype(BF16)
        for c in range(2 * D_FF // lead_cols):
            cols = slice(c * lead_cols, (c + 1) * lead_cols)
            ul = jnp.dot(hl, wup_ref[:, cols], preferred_element_type=F32)
            carry_ref[0:SUBLANES, cols] = pltpu.roll(ul, 1, 0)
            carry_ref[SUBLANES:2 * SUBLANES, cols] = ul

    n_slab = h_ref.shape[1] // LANES
    for j in range(n_slab):
        perm_ref[j] = h_ref[:, j * LANES:(j + 1) * LANES]
    h = jnp.concatenate(
        [jnp.concatenate([perm_ref[j, pl.ds(k, SUBLANES, stride=nv), :] for k in range(nv)],
                         axis=0) for j in range(n_slab)], axis=1)
    hn = _rms_rows(h, g2_ref[...]).astype(BF16)
    sub = lax.broadcasted_iota(jnp.int32, (SUBLANES, FF_CHUNK), 0)

    def up_cols(lo):
        return jnp.dot(hn, wup_ref[:, lo:lo + FF_CHUNK], preferred_element_type=F32)

    def conv_cols(u, lo):
        cols = slice(lo, lo + FF_CHUNK)
        prev = carry_ref[:, cols]
        carry_ref[:, cols] = u[tm - 2 * SUBLANES:tm, :]

        def wrap(prev8, last8):
            return jnp.where(sub == 0, pltpu.roll(prev8, 1, 0), pltpu.roll(last8, 1, 0))

        f1 = wrap(prev[SUBLANES:, :], u[tm - SUBLANES:tm, :])
        f2 = wrap(prev[:SUBLANES, :], u[tm - 2 * SUBLANES:tm - SUBLANES, :])
        s1 = jnp.concatenate([f1, u[:tm - SUBLANES, :]], axis=0)
        s2 = jnp.concatenate([f2, f1, u[:tm - 2 * SUBLANES, :]], axis=0)
        cw = cw_ref[:, cols]
        return cw[0:1, :] * s2 + cw[1:2, :] * s1 + cw[2:3, :] * u + cb_ref[:, cols]

    n_chunks = D_FF // FF_CHUNK
    nxt = (up_cols(0), up_cols(D_FF))
    acc = h
    for c in range(n_chunks):
        u_g, u_v = nxt
        if c + 1 < n_chunks:
            nxt = (up_cols((c + 1) * FF_CHUNK), up_cols(D_FF + (c + 1) * FF_CHUNK))
        u_gate = conv_cols(u_g, c * FF_CHUNK)
        u_val = conv_cols(u_v, D_FF + c * FF_CHUNK)
        act = (u_gate * _sigmoid(u_gate) * u_val).astype(BF16)
        acc = acc + jnp.dot(act, wdn_ref[c * FF_CHUNK:(c + 1) * FF_CHUNK, :],
                            preferred_element_type=F32)
    for j in range(n_slab):
        for k in range(nv):
            perm_ref[j, pl.ds(k, SUBLANES, stride=nv), :] = acc[
                k * SUBLANES:(k + 1) * SUBLANES, j * LANES:(j + 1) * LANES]
        o_ref[:, j * LANES:(j + 1) * LANES] = perm_ref[j]


def _row_spec(width, last=None):
    return pl.BlockSpec((ROW_TILE, width), lambda i: (_clamp(i, last), 0))


def kernel(x, meta_tokens, norm1_gain, w_in, fox_b_f, q_norm_gain, k_norm_gain,
           hg_lb_logits, hg_out_gain, w_branch_a, w_branch_b, w_out, norm2_gain,
           w_up, conv_w, conv_b, w_down):
    batch, seq, d = x.shape
    lp = LEAD + seq
    rows = batch * lp
    assert d == D_MODEL and lp % ROW_TILE == 0 and seq % ATT_BLOCK == 0
    assert lp % (HG_CHUNK * HG_UNROLL) == 0 and D_FF % FF_CHUNK == 0
    n_tiles = rows // ROW_TILE
    layer = 0

    xf = x.astype(F32)
    meta = meta_tokens.astype(F32)
    tiles_per_seq = lp // ROW_TILE

    w = w_in[layer]
    o_fq, o_fk, o_fv, o_ff = 0, 512, 1024, 1536
    o_hq = o_ff + FOX_HEADS
    o_hf, o_hi, o_hg = o_hq + 512, o_hq + 1024, o_hq + 1536
    o_ga = o_hq + 2048
    o_gb = o_ga + D_MODEL
    w1 = jnp.concatenate([
        w[:, o_fq:o_fk], w[:, o_fk:o_fv], w[:, o_fv:o_ff],
        w[:, o_hq:o_hf], w[:, o_hf:o_hi], w[:, o_hi:o_hg], w[:, o_hg:o_ga],
        w[:, o_ga:o_gb], w[:, o_gb:o_gb + D_MODEL],
        w[:, o_ff:o_hq], jnp.zeros((d, LANES - FOX_HEADS), w.dtype)], axis=1).astype(BF16)
    bf = jnp.concatenate([fox_b_f[layer].astype(F32),
                          jnp.zeros((LANES - FOX_HEADS,), F32)]).reshape(1, LANES)
    scale = FOX_HEAD_DIM ** -0.5 * LOG2E
    gq = (jnp.tile(q_norm_gain[layer].astype(F32), FOX_HEADS) * scale).reshape(1, FOX_WIDTH)
    gk = jnp.tile(k_norm_gain[layer].astype(F32), FOX_HEADS).reshape(1, FOX_WIDTH)
    lower_bounds = jnp.cumsum(jax.nn.softmax(hg_lb_logits.astype(F32), axis=0), axis=0)
    lb = lower_bounds[layer].reshape(1, HG_WIDTH)
    head_id = jnp.arange(FOX_WIDTH) // FOX_HEAD_DIM
    gm = jnp.where(head_id[:, None] == head_id[None, :], 1.0 / FOX_HEAD_DIM, 0.0).astype(BF16)

    bf16_w = lambda width: jax.ShapeDtypeStruct((rows, width), BF16)
    outs = pl.pallas_call(
        functools.partial(_proj_kernel, tiles_per_seq=tiles_per_seq),
        grid=(n_tiles,),
        in_specs=[_frame_window_spec(tiles_per_seq, ROW_TILE, d), _resident((N_META, d)),
                  _resident((1, d)), _resident((d, C_END)),
                  _resident((1, LANES)), _resident((1, FOX_WIDTH)), _resident((1, FOX_WIDTH)),
                  _resident((1, HG_WIDTH)), _resident((FOX_WIDTH, FOX_WIDTH))],
        out_specs=[_row_spec(FOX_WIDTH), _row_spec(FOX_WIDTH), _row_spec(FOX_WIDTH),
                   _row_spec(FOX_HEADS),
                   _row_spec(HG_WIDTH), _row_spec(HG_WIDTH), _row_spec(HG_WIDTH),
                   _row_spec(HG_WIDTH), _row_spec(HG_WIDTH),
                   _row_spec(d), _row_spec(d)],
        out_shape=[bf16_w(FOX_WIDTH), bf16_w(FOX_WIDTH), bf16_w(FOX_WIDTH),
                   jax.ShapeDtypeStruct((rows, FOX_HEADS), F32),
                   bf16_w(HG_WIDTH), bf16_w(HG_WIDTH), bf16_w(HG_WIDTH), bf16_w(HG_WIDTH),
                   jax.ShapeDtypeStruct((rows, HG_WIDTH), F32),
                   bf16_w(d), bf16_w(d)],
        scratch_shapes=[pltpu.VMEM((1, LANES), F32)],
        compiler_params=pltpu.CompilerParams(
            dimension_semantics=("arbitrary",), vmem_limit_bytes=52 * 1024 * 1024),
        name="proj",
    )(xf, meta, norm1_gain[layer].astype(F32).reshape(1, d), w1, bf, gq, gk, lb, gm)
    fq, fk, fv, cum, hq, kin, hi, hgs, glog, ga, gb = outs

    n_grp = FOX_HEADS // HEADS_PER_GROUP
    n_real = seq // ATT_BLOCK
    cum_row = jnp.transpose(cum.reshape(batch, lp, n_grp, HEADS_PER_GROUP), (0, 2, 3, 1))
    cum_row_lead = cum_row[..., :LEAD]
    n_kb = seq // KEY_BLOCK
    cum_row_real = jnp.transpose(
        cum_row[..., LEAD:].reshape(batch, n_grp, HEADS_PER_GROUP, n_kb, KEY_BLOCK),
        (0, 1, 3, 2, 4))
    seq_spec = pl.BlockSpec((lp, GROUP_LANES), lambda b, g: (b, g))
    o_fox = pl.pallas_call(
        functools.partial(_attn_kernel, n_real=n_real),
        grid=(batch, n_grp),
        in_specs=[seq_spec, seq_spec, seq_spec,
                  pl.BlockSpec((None, None, HEADS_PER_GROUP, LEAD), lambda b, g: (b, g, 0, 0)),
                  pl.BlockSpec((None, None, n_kb, HEADS_PER_GROUP, KEY_BLOCK),
                               lambda b, g: (b, g, 0, 0, 0))],
        out_specs=seq_spec,
        out_shape=jax.ShapeDtypeStruct((rows, FOX_WIDTH), BF16),
        scratch_shapes=[pltpu.VMEM((HEADS_PER_GROUP, ATT_BLOCK, GROUP_LANES), BF16),
                        pltpu.VMEM((HEADS_PER_GROUP, ATT_BLOCK, GROUP_LANES), F32),
                        pltpu.VMEM((HEADS_PER_GROUP, ATT_BLOCK, LANES), F32)],
        compiler_params=pltpu.CompilerParams(
            dimension_semantics=("parallel", "parallel"), vmem_limit_bytes=48 * 1024 * 1024),
        name="fox_attn",
    )(fq, fk, fv, cum_row_lead, cum_row_real)

    head_spec = pl.BlockSpec((lp, HG_GROUP * HG_DIM), lambda b, hh: (b, hh))
    o_hg = pl.pallas_call(
        functools.partial(_hgrn_kernel, n_chunk=lp // HG_CHUNK),
        grid=(batch, HG_HEADS // HG_GROUP),
        in_specs=[head_spec, head_spec, head_spec, head_spec, head_spec,
                  pl.BlockSpec((1, HG_DIM), lambda b, hh: (0, 0))],
        out_specs=head_spec,
        out_shape=jax.ShapeDtypeStruct((rows, HG_WIDTH), BF16),
        scratch_shapes=[pltpu.VMEM((HG_GROUP, HG_DIM, HG_DIM), F32)],
        compiler_params=pltpu.CompilerParams(
            dimension_semantics=("parallel", "parallel"), vmem_limit_bytes=40 * 1024 * 1024),
        name="hgrn2",
    )(hq, kin, hi, hgs, glog, hg_out_gain[layer].astype(F32).reshape(1, HG_DIM))

    last = n_tiles - 1
    h1 = pl.pallas_call(
        functools.partial(_mix_kernel, tiles_per_seq=tiles_per_seq),
        grid=(n_tiles + 1,),
        in_specs=[_frame_window_spec(tiles_per_seq, ROW_TILE, d, last), _resident((N_META, d)),
                  _row_spec(HG_WIDTH, last), _row_spec(FOX_WIDTH, last),
                  _row_spec(d, last), _row_spec(d, last),
                  _resident((HG_WIDTH, d)), _resident((FOX_WIDTH, d)), _resident((d, d))],
        out_specs=_row_spec(d),
        out_shape=jax.ShapeDtypeStruct((rows + ROW_TILE, d), F32),
        compiler_params=pltpu.CompilerParams(
            dimension_semantics=("parallel",), vmem_limit_bytes=40 * 1024 * 1024),
        name="mix",
    )(xf, meta, o_hg, o_fox, ga, gb, w_branch_a[layer].astype(BF16),
      w_branch_b[layer].astype(BF16), w_out[layer].astype(BF16))

    h2 = pl.pallas_call(
        _ffn_kernel,
        grid=(batch, tiles_per_seq),
        in_specs=[pl.BlockSpec((pl.Element(ROW_TILE), pl.Element(d)),
                               lambda b, t: ((b * (lp // LEAD) + 1 + t * (ROW_TILE // LEAD))
                                             * LEAD, 0)),
                  pl.BlockSpec((pl.Element(SUBLANES), pl.Element(d)),
                               lambda b, t: (LEAD - SUBLANES, 0)),
                  _resident((1, d)), _resident((d, 2 * D_FF)),
                  _resident((CONV_WIDTH, 2 * D_FF)), _resident((1, 2 * D_FF)),
                  _resident((D_FF, d))],
        out_specs=pl.BlockSpec((None, ROW_TILE, d), lambda b, t: (b, t, 0)),
        out_shape=jax.ShapeDtypeStruct((batch, seq, d), F32),
        scratch_shapes=[pltpu.VMEM((2 * SUBLANES, 2 * D_FF), F32),
                        pltpu.VMEM((d // LANES, ROW_TILE, LANES), F32)],
        compiler_params=pltpu.CompilerParams(
            dimension_semantics=("arbitrary", "arbitrary"),
            vmem_limit_bytes=52 * 1024 * 1024),
        name="ffn",
    )(h1, h1, norm2_gain[layer].astype(F32).reshape(1, d), w_up[layer].astype(BF16),
      conv_w[layer].astype(F32), conv_b[layer].astype(F32).reshape(1, 2 * D_FF),
      w_down[layer].astype(BF16))

    return h2.astype(x.dtype)
```

```python
import functools

import jax
import jax.numpy as jnp
from jax import lax
from jax.experimental import pallas as pl
from jax.experimental.pallas import tpu as pltpu

F32 = jnp.float32
BF16 = jnp.bfloat16

D_MODEL = 1024
N_META = 16
FOX_HEADS = 8
FOX_HEAD_DIM = 64
FOX_WIDTH = FOX_HEADS * FOX_HEAD_DIM
HG_HEADS = 4
HG_DIM = 128
HG_WIDTH = HG_HEADS * HG_DIM
D_FF = 2816
CONV_WIDTH = 3
EPS = 1e-6

LANES = 128
SUBLANES = 8
LEAD = 128
PAD_ROWS = LEAD - N_META
ROW_TILE = 384
GROUP_LANES = 256
HEADS_PER_GROUP = GROUP_LANES // FOX_HEAD_DIM
ATT_BLOCK = 1024
KEY_BLOCK = 256
KEY_UNROLL = 4
LOG2E = 1.4426950408889634
HG_CHUNK = 128
HG_BASE = 16
HG_GROUP = 2
HG_UNROLL = 3
FF_CHUNK = 256
MASK_VALUE = -1e30

C_FQ, C_FK, C_FV, C_HQ, C_HF, C_HI, C_HG, C_GA, C_GB, C_FF, C_END = (
    0, 512, 1024, 1536, 2048, 2560, 3072, 3584, 4608, 5632, 5760)

NT_DIMS = (((1,), (1,)), ((), ()))


def _sigmoid(x):
    return 1.0 / (1.0 + jnp.exp(-x))


def _rms_rows(x, gain):
    ms = jnp.mean(x * x, axis=-1, keepdims=True)
    return x * lax.rsqrt(ms + EPS) * gain


def _padded_tile(x_ref, meta_ref, tile_in_seq):
    xb = x_ref[...]
    tm, d = xb.shape
    lead = jnp.concatenate([jnp.zeros((PAD_ROWS, d), F32), meta_ref[...]], axis=0)
    first = jnp.concatenate([lead, xb[:tm - LEAD, :]], axis=0)
    return jnp.where(tile_in_seq == 0, first, xb)


def _frame_window_spec(tiles_per_seq, tm, d):
    assert tm % LEAD == 0
    return pl.BlockSpec(
        (None, pl.Element(tm), pl.Element(d)),
        lambda i: (i // tiles_per_seq,
                   jnp.maximum((i % tiles_per_seq) * (tm // LEAD) - 1, 0) * LEAD, 0))


def _resident(shape):
    nd = len(shape)
    return pl.BlockSpec(shape, lambda *_: (0,) * nd, pipeline_mode=pl.Buffered(1))


def _proj_kernel(x_ref, meta_ref, g1_ref, w_ref, bf_ref, gq_ref, gk_ref, lb_ref, gm_ref,
                 q_ref, k_ref, v_ref, cum_ref, hq_ref, kin_ref, hi_ref, hgs_ref,
                 glog_ref, ga_ref, gb_ref, carry_ref, *, tiles_per_seq):
    tm = x_ref.shape[0]
    tile_in_seq = pl.program_id(0) % tiles_per_seq

    @pl.when(tile_in_seq == 0)
    def _():
        carry_ref[...] = jnp.zeros_like(carry_ref)

    xn = _rms_rows(_padded_tile(x_ref, meta_ref, tile_in_seq), g1_ref[...]).astype(BF16)

    def proj(a, b):
        return jnp.dot(xn, w_ref[:, a:b], preferred_element_type=F32)

    gm = gm_ref[...]

    def head_norm(y, gain):
        msq = jnp.dot((y * y).astype(BF16), gm, preferred_element_type=F32)
        return y * lax.rsqrt(msq + EPS) * gain

    q_ref[...] = head_norm(proj(C_FQ, C_FK), gq_ref[...]).astype(BF16)
    k_ref[...] = head_norm(proj(C_FK, C_FV), gk_ref[...]).astype(BF16)
    v_ref[...] = proj(C_FV, C_HQ).astype(BF16)

    z = proj(C_FF, C_END) + bf_ref[...]
    logf = -(jnp.maximum(-z, 0.0) + jnp.log(1.0 + jnp.exp(-jnp.abs(z))))
    rowi = lax.broadcasted_iota(jnp.int32, (tm, LANES), 0)
    cum = logf
    step = 1
    while step < tm:
        cum = cum + jnp.where(rowi >= step, pltpu.roll(cum, step, 0), 0.0)
        step *= 2
    cum = cum + carry_ref[...]
    cum_ref[...] = cum[:, :FOX_HEADS]
    carry_ref[...] = cum[tm - 1:tm, :]

    hq_ref[...] = proj(C_HQ, C_HF).astype(BF16)
    hf = proj(C_HF, C_HI)
    t = jnp.exp(-jnp.abs(hf))
    r = 1.0 / (1.0 + t)
    tr = t * r
    pos = hf >= 0.0
    sig_pos = jnp.where(pos, r, tr)
    sig_neg = jnp.where(pos, tr, r)
    lb = lb_ref[...]
    glog_ref[...] = jnp.log2(lb + (1.0 - lb) * sig_pos)
    kin_ref[...] = ((1.0 - lb) * sig_neg).astype(BF16)
    hi_ref[...] = proj(C_HI, C_HG).astype(BF16)
    hg = proj(C_HG, C_GA)
    hgs_ref[...] = (hg * _sigmoid(hg)).astype(BF16)

    ga_ref[...] = _sigmoid(proj(C_GA, C_GB)).astype(BF16)
    gb_ref[...] = _sigmoid(proj(C_GB, C_FF)).astype(BF16)


def _attn_kernel(q_ref, k_ref, v_ref, crm_ref, crr_ref, o_ref,
                 qm_ref, acc_ref, m_ref, *, n_real):
    hp = HEADS_PER_GROUP
    lane_head = lax.broadcasted_iota(jnp.int32, (1, GROUP_LANES), 1) // FOX_HEAD_DIM

    def setup(qs, tq):
        qb = q_ref[pl.ds(qs, tq), :]
        for j in range(hp):
            qm_ref[j, :tq, :] = jnp.where(lane_head == j, qb, jnp.zeros_like(qb))
            m_ref[j, :tq, :] = jnp.full((tq, LANES), MASK_VALUE, F32)
            acc_ref[j, :tq, :] = jnp.zeros((tq, GROUP_LANES), F32)

    def block(r0, nr, ks, tk, c0, cr, mask):
        kb = k_ref[pl.ds(ks, tk), :]
        vb = v_ref[pl.ds(ks, tk), :]
        bias = (c0 - cr) * LOG2E
        rows = slice(r0, r0 + nr)
        for j in range(hp):
            s = lax.dot_general(qm_ref[j, rows, :], kb, NT_DIMS,
                                preferred_element_type=F32)
            s = s + bias[j:j + 1, :]
            if mask is not None:
                s = jnp.where(mask, s, MASK_VALUE)
            m_old = m_ref[j, rows, :]
            m_new = jnp.maximum(m_old, jnp.max(s, axis=-1, keepdims=True))
            alpha = jnp.exp2(m_old - m_new)
            p = jnp.exp2(s - jnp.concatenate([m_new] * (tk // LANES), axis=1))
            vj = jnp.where(lane_head == j, vb,
                           jnp.where(lane_head == (j + 1) % hp, 1.0, 0.0).astype(BF16))
            acc_ref[j, rows, :] = (
                jnp.concatenate([alpha] * (GROUP_LANES // LANES), axis=1) * acc_ref[j, rows, :]
                + jnp.dot(p.astype(BF16), vj, preferred_element_type=F32))
            m_ref[j, rows, :] = m_new

    def finalize(qs, tq):
        out = jnp.zeros((tq, GROUP_LANES), F32)
        for j in range(hp):
            acc = acc_ref[j, :tq, :]
            c = ((j + 1) % hp) * FOX_HEAD_DIM
            out = jnp.where(lane_head == j, acc * (1.0 / acc[:, c:c + 1]), out)
        o_ref[pl.ds(qs, tq), :] = out.astype(BF16)

    crm = crm_ref[...]
    row_m = lax.broadcasted_iota(jnp.int32, (LEAD, LEAD), 0)
    col_m = lax.broadcasted_iota(jnp.int32, (LEAD, LEAD), 1)
    setup(0, LEAD)
    block(0, LEAD, 0, LEAD, crm[:, PAD_ROWS:PAD_ROWS + 1], crm,
          (col_m >= PAD_ROWS) & (col_m <= row_m))
    finalize(0, LEAD)

    tq, tk = ATT_BLOCK, KEY_BLOCK
    kpq = tq // tk
    lead_mask = lax.broadcasted_iota(jnp.int32, (tq, LEAD), 1) >= PAD_ROWS

    def q_step(i, carry):
        qs = pl.multiple_of(LEAD + i * tq, LANES)
        c0 = crr_ref[i * kpq][:, 0:1]
        setup(qs, tq)
        block(0, tq, 0, LEAD, c0, crm, lead_mask)

        def k_step(jq, c):
            for d in range(KEY_UNROLL):
                jb = jq * KEY_UNROLL + d
                ks = pl.multiple_of(LEAD + jb * tk, LANES)
                block(0, tq, ks, tk, c0, crr_ref[jb], None)
            return c

        lax.fori_loop(0, i * (kpq // KEY_UNROLL), k_step, 0)
        for d in range(kpq):
            causal = (lax.broadcasted_iota(jnp.int32, (tq, tk), 1) + d * tk
                      <= lax.broadcasted_iota(jnp.int32, (tq, tk), 0))
            block(0, tq, pl.multiple_of(qs + d * tk, LANES), tk, c0,
                  crr_ref[i * kpq + d], causal)
        finalize(qs, tq)
        return carry

    lax.fori_loop(0, n_real, q_step, 0)


def _hgrn_kernel(hq_ref, kin_ref, hi_ref, hgs_ref, gl_ref, gain_ref, o_ref,
                 st_ref, *, n_chunk):
    c_len = HG_CHUNK
    st_ref[...] = jnp.zeros_like(st_ref)
    row = lax.broadcasted_iota(jnp.int32, (c_len, c_len), 0)
    col = lax.broadcasted_iota(jnp.int32, (c_len, c_len), 1)
    rowv = lax.broadcasted_iota(jnp.int32, (c_len, HG_DIM), 0)
    gain = gain_ref[...]

    def ref_rows(g, period, offset):
        parts = [jnp.broadcast_to(g[a * period + offset:a * period + offset + 1, :],
                                  (period, HG_DIM))
                 for a in range(c_len // period)]
        return jnp.concatenate(parts, axis=0)

    def chunk(c, hh):
        rows = pl.ds(pl.multiple_of(c * c_len, c_len), c_len)
        cols = slice(hh * HG_DIM, (hh + 1) * HG_DIM)
        g = gl_ref[rows, cols]
        d = 1
        while d < c_len:
            g = g + jnp.where(rowv >= d, pltpu.roll(g, d, 0), 0.0)
            d *= 2
        hq = hq_ref[rows, cols].astype(F32)
        kin = kin_ref[rows, cols].astype(F32)
        v = hi_ref[rows, cols]

        a_mat = jnp.zeros((c_len, c_len), F32)
        bs = c_len // 2
        while bs >= HG_BASE:
            ref = ref_rows(g, 2 * bs, bs - 1)
            ql = (hq * jnp.exp2(jnp.minimum(g - ref, 0.0))).astype(BF16)
            kl = (kin * jnp.exp2(jnp.minimum(ref - g, 0.0))).astype(BF16)
            al = lax.dot_general(ql, kl, NT_DIMS, preferred_element_type=F32)
            same = (row // (2 * bs)) == (col // (2 * bs))
            ml = same & ((row & bs) != 0) & ((col & bs) == 0)
            a_mat = jnp.where(ml, al, a_mat)
            bs //= 2
        ref = ref_rows(g, HG_BASE, HG_BASE // 2 - 1)
        qd = (hq * jnp.exp2(g - ref)).astype(BF16)
        kd = (kin * jnp.exp2(ref - g)).astype(BF16)
        ad = lax.dot_general(qd, kd, NT_DIMS, preferred_element_type=F32)
        md = ((row // HG_BASE) == (col // HG_BASE)) & (col <= row)
        a_mat = jnp.where(md, ad, a_mat)

        st = st_ref[hh]
        o = jnp.dot(a_mat.astype(BF16), v, preferred_element_type=F32)
        o = o + lax.dot_general((hq * jnp.exp2(g)).astype(BF16), st.astype(BF16),
                                NT_DIMS, preferred_element_type=F32)

        g_tot = g[c_len - 1:c_len, :]
        kt = (kin * jnp.exp2(g_tot - g)).astype(BF16)
        v_t = v.astype(F32).T.astype(BF16)
        st_ref[hh] = st * jnp.exp2(g_tot) + jnp.dot(v_t, kt, preferred_element_type=F32)

        y = _rms_rows(o, gain) * hgs_ref[rows, cols].astype(F32)
        o_ref[rows, cols] = y.astype(BF16)

    def body(cu, carry):
        for u in range(HG_UNROLL):
            for hh in range(HG_GROUP):
                chunk(cu * HG_UNROLL + u, hh)
        return carry

    lax.fori_loop(0, n_chunk // HG_UNROLL, body, 0)


def _mix_ffn_kernel(x_ref, meta_ref, ohg_ref, ofox_ref, ga_ref, gb_ref,
                    lohg_ref, lofox_ref, lga_ref, lgb_ref,
                    wa_ref, wb_ref, wo_ref, g2_ref, wup_ref, cw_ref, cb_ref, wdn_ref,
                    o_ref, carry_ref, perm_ref, *, tiles_per_seq):
    tm, d = x_ref.shape
    nv = tm // SUBLANES
    lead_cols = 2 * FF_CHUNK
    t = pl.program_id(1)

    def mix(h, ohg, ofox, ga, gb):
        y_a = jnp.dot(ohg, wa_ref[...], preferred_element_type=F32)
        y_b = jnp.dot(ofox, wb_ref[...], preferred_element_type=F32)
        merged = ga.astype(F32) * y_a + gb.astype(F32) * y_b
        return h + jnp.dot(merged.astype(BF16), wo_ref[...], preferred_element_type=F32)

    @pl.when((pl.program_id(0) == 0) & (t == 0))
    def _():
        hl = mix(meta_ref[...], lohg_ref[...], lofox_ref[...], lga_ref[...], lgb_ref[...])
        hl = _rms_rows(hl, g2_ref[...]).astype(BF16)
        for c in range(2 * D_FF // lead_cols):
            cols = slice(c * lead_cols, (c + 1) * lead_cols)
            ul = jnp.dot(hl, wup_ref[:, cols], preferred_element_type=F32)[SUBLANES:, :]
            carry_ref[0:SUBLANES, cols] = pltpu.roll(ul, 1, 0)
            carry_ref[SUBLANES:2 * SUBLANES, cols] = ul

    xb = x_ref[...]
    lead = jnp.concatenate([jnp.zeros((PAD_ROWS, d), F32), meta_ref[...]], axis=0)
    h_in = jnp.where(t == tiles_per_seq - 1,
                     jnp.concatenate([xb[:tm - LEAD, :], lead], axis=0), xb)

    pulled_back = (pl.program_id(0) == pl.num_programs(0) - 1) & (t == tiles_per_seq - 1)

    def aligned(ref):
        w = ref[...]
        return jnp.where(pulled_back, jnp.concatenate([w[LEAD:, :], w[:LEAD, :]], axis=0), w)

    h1 = mix(h_in, aligned(ohg_ref), aligned(ofox_ref), aligned(ga_ref), aligned(gb_ref))

    n_slab = d // LANES
    for j in range(n_slab):
        perm_ref[j] = h1[:, j * LANES:(j + 1) * LANES]
    h = jnp.concatenate(
        [jnp.concatenate([perm_ref[j, pl.ds(k, SUBLANES, stride=nv), :] for k in range(nv)],
                         axis=0) for j in range(n_slab)], axis=1)
    hn = _rms_rows(h, g2_ref[...]).astype(BF16)
    sub = lax.broadcasted_iota(jnp.int32, (SUBLANES, FF_CHUNK), 0)

    def up_cols(lo):
        return jnp.dot(hn, wup_ref[:, lo:lo + FF_CHUNK], preferred_element_type=F32)

    def conv_cols(u, lo):
        cols = slice(lo, lo + FF_CHUNK)
        prev = carry_ref[:, cols]
        carry_ref[:, cols] = u[tm - 2 * SUBLANES:tm, :]

        def wrap(prev8, last8):
            return jnp.where(sub == 0, pltpu.roll(prev8, 1, 0), pltpu.roll(last8, 1, 0))

        f1 = wrap(prev[SUBLANES:, :], u[tm - SUBLANES:tm, :])
        f2 = wrap(prev[:SUBLANES, :], u[tm - 2 * SUBLANES:tm - SUBLANES, :])
        s1 = jnp.concatenate([f1, u[:tm - SUBLANES, :]], axis=0)
        s2 = jnp.concatenate([f2, f1, u[:tm - 2 * SUBLANES, :]], axis=0)
        cw = cw_ref[:, cols]
        return cw[0:1, :] * s2 + cw[1:2, :] * s1 + cw[2:3, :] * u + cb_ref[:, cols]

    n_chunks = D_FF // FF_CHUNK
    nxt = (up_cols(0), up_cols(D_FF))
    acc = h
    for c in range(n_chunks):
        u_g, u_v = nxt
        if c + 1 < n_chunks:
            nxt = (up_cols((c + 1) * FF_CHUNK), up_cols(D_FF + (c + 1) * FF_CHUNK))
        u_gate = conv_cols(u_g, c * FF_CHUNK)
        u_val = conv_cols(u_v, D_FF + c * FF_CHUNK)
        act = (u_gate * _sigmoid(u_gate) * u_val).astype(BF16)
        acc = acc + jnp.dot(act, wdn_ref[c * FF_CHUNK:(c + 1) * FF_CHUNK, :],
                            preferred_element_type=F32)
    for j in range(n_slab):
        for k in range(nv):
            perm_ref[j, pl.ds(k, SUBLANES, stride=nv), :] = acc[
                k * SUBLANES:(k + 1) * SUBLANES, j * LANES:(j + 1) * LANES]
        o_ref[:, j * LANES:(j + 1) * LANES] = perm_ref[j]


def _row_spec(width):
    return pl.BlockSpec((ROW_TILE, width), lambda i: (i, 0))


def kernel(x, meta_tokens, norm1_gain, w_in, fox_b_f, q_norm_gain, k_norm_gain,
           hg_lb_logits, hg_out_gain, w_branch_a, w_branch_b, w_out, norm2_gain,
           w_up, conv_w, conv_b, w_down):
    batch, seq, d = x.shape
    lp = LEAD + seq
    rows = batch * lp
    assert d == D_MODEL and lp % ROW_TILE == 0 and seq % ATT_BLOCK == 0
    assert lp % (HG_CHUNK * HG_UNROLL) == 0 and D_FF % FF_CHUNK == 0
    n_tiles = rows // ROW_TILE
    layer = 0

    xf = x.astype(F32)
    meta = meta_tokens.astype(F32)
    tiles_per_seq = lp // ROW_TILE

    w = w_in[layer]
    o_fq, o_fk, o_fv, o_ff = 0, 512, 1024, 1536
    o_hq = o_ff + FOX_HEADS
    o_hf, o_hi, o_hg = o_hq + 512, o_hq + 1024, o_hq + 1536
    o_ga = o_hq + 2048
    o_gb = o_ga + D_MODEL
    w1 = jnp.concatenate([
        w[:, o_fq:o_fk], w[:, o_fk:o_fv], w[:, o_fv:o_ff],
        w[:, o_hq:o_hf], w[:, o_hf:o_hi], w[:, o_hi:o_hg], w[:, o_hg:o_ga],
        w[:, o_ga:o_gb], w[:, o_gb:o_gb + D_MODEL],
        w[:, o_ff:o_hq], jnp.zeros((d, LANES - FOX_HEADS), w.dtype)], axis=1).astype(BF16)
    bf = jnp.concatenate([fox_b_f[layer].astype(F32),
                          jnp.zeros((LANES - FOX_HEADS,), F32)]).reshape(1, LANES)
    scale = FOX_HEAD_DIM ** -0.5 * LOG2E
    gq = (jnp.tile(q_norm_gain[layer].astype(F32), FOX_HEADS) * scale).reshape(1, FOX_WIDTH)
    gk = jnp.tile(k_norm_gain[layer].astype(F32), FOX_HEADS).reshape(1, FOX_WIDTH)
    lower_bounds = jnp.cumsum(jax.nn.softmax(hg_lb_logits.astype(F32), axis=0), axis=0)
    lb = lower_bounds[layer].reshape(1, HG_WIDTH)
    head_id = jnp.arange(FOX_WIDTH) // FOX_HEAD_DIM
    gm = jnp.where(head_id[:, None] == head_id[None, :], 1.0 / FOX_HEAD_DIM, 0.0).astype(BF16)

    bf16_w = lambda width: jax.ShapeDtypeStruct((rows, width), BF16)
    outs = pl.pallas_call(
        functools.partial(_proj_kernel, tiles_per_seq=tiles_per_seq),
        grid=(n_tiles,),
        in_specs=[_frame_window_spec(tiles_per_seq, ROW_TILE, d), _resident((N_META, d)),
                  _resident((1, d)), _resident((d, C_END)),
                  _resident((1, LANES)), _resident((1, FOX_WIDTH)), _resident((1, FOX_WIDTH)),
                  _resident((1, HG_WIDTH)), _resident((FOX_WIDTH, FOX_WIDTH))],
        out_specs=[_row_spec(FOX_WIDTH), _row_spec(FOX_WIDTH), _row_spec(FOX_WIDTH),
                   _row_spec(FOX_HEADS),
                   _row_spec(HG_WIDTH), _row_spec(HG_WIDTH), _row_spec(HG_WIDTH),
                   _row_spec(HG_WIDTH), _row_spec(HG_WIDTH),
                   _row_spec(d), _row_spec(d)],
        out_shape=[bf16_w(FOX_WIDTH), bf16_w(FOX_WIDTH), bf16_w(FOX_WIDTH),
                   jax.ShapeDtypeStruct((rows, FOX_HEADS), F32),
                   bf16_w(HG_WIDTH), bf16_w(HG_WIDTH), bf16_w(HG_WIDTH), bf16_w(HG_WIDTH),
                   jax.ShapeDtypeStruct((rows, HG_WIDTH), F32),
                   bf16_w(d), bf16_w(d)],
        scratch_shapes=[pltpu.VMEM((1, LANES), F32)],
        compiler_params=pltpu.CompilerParams(
            dimension_semantics=("arbitrary",), vmem_limit_bytes=52 * 1024 * 1024),
        name="proj",
    )(xf, meta, norm1_gain[layer].astype(F32).reshape(1, d), w1, bf, gq, gk, lb, gm)
    fq, fk, fv, cum, hq, kin, hi, hgs, glog, ga, gb = outs

    n_grp = FOX_HEADS // HEADS_PER_GROUP
    n_real = seq // ATT_BLOCK
    cum_row = jnp.transpose(cum.reshape(batch, lp, n_grp, HEADS_PER_GROUP), (0, 2, 3, 1))
    cum_row_lead = cum_row[..., :LEAD]
    n_kb = seq // KEY_BLOCK
    cum_row_real = jnp.transpose(
        cum_row[..., LEAD:].reshape(batch, n_grp, HEADS_PER_GROUP, n_kb, KEY_BLOCK),
        (0, 1, 3, 2, 4))
    seq_spec = pl.BlockSpec((lp, GROUP_LANES), lambda b, g: (b, g))
    o_fox = pl.pallas_call(
        functools.partial(_attn_kernel, n_real=n_real),
        grid=(batch, n_grp),
        in_specs=[seq_spec, seq_spec, seq_spec,
                  pl.BlockSpec((None, None, HEADS_PER_GROUP, LEAD), lambda b, g: (b, g, 0, 0)),
                  pl.BlockSpec((None, None, n_kb, HEADS_PER_GROUP, KEY_BLOCK),
                               lambda b, g: (b, g, 0, 0, 0))],
        out_specs=seq_spec,
        out_shape=jax.ShapeDtypeStruct((rows, FOX_WIDTH), BF16),
        scratch_shapes=[pltpu.VMEM((HEADS_PER_GROUP, ATT_BLOCK, GROUP_LANES), BF16),
                        pltpu.VMEM((HEADS_PER_GROUP, ATT_BLOCK, GROUP_LANES), F32),
                        pltpu.VMEM((HEADS_PER_GROUP, ATT_BLOCK, LANES), F32)],
        compiler_params=pltpu.CompilerParams(
            dimension_semantics=("parallel", "parallel"), vmem_limit_bytes=48 * 1024 * 1024),
        name="fox_attn",
    )(fq, fk, fv, cum_row_lead, cum_row_real)

    head_spec = pl.BlockSpec((lp, HG_GROUP * HG_DIM), lambda b, hh: (b, hh))
    o_hg = pl.pallas_call(
        functools.partial(_hgrn_kernel, n_chunk=lp // HG_CHUNK),
        grid=(batch, HG_HEADS // HG_GROUP),
        in_specs=[head_spec, head_spec, head_spec, head_spec, head_spec,
                  pl.BlockSpec((1, HG_DIM), lambda b, hh: (0, 0))],
        out_specs=head_spec,
        out_shape=jax.ShapeDtypeStruct((rows, HG_WIDTH), BF16),
        scratch_shapes=[pltpu.VMEM((HG_GROUP, HG_DIM, HG_DIM), F32)],
        compiler_params=pltpu.CompilerParams(
            dimension_semantics=("parallel", "parallel"), vmem_limit_bytes=40 * 1024 * 1024),
        name="hgrn2",
    )(hq, kin, hi, hgs, glog, hg_out_gain[layer].astype(F32).reshape(1, HG_DIM))

    last_start = (rows - ROW_TILE) // LEAD

    def window(width):
        return pl.BlockSpec(
            (pl.Element(ROW_TILE), pl.Element(width)),
            lambda b, t: (jnp.minimum(b * (lp // LEAD) + 1 + t * (ROW_TILE // LEAD), last_start)
                          * LEAD, 0))

    def meta_rows(width):
        return pl.BlockSpec((pl.Element(N_META), pl.Element(width)),
                            lambda b, t: (PAD_ROWS, 0))

    h2 = pl.pallas_call(
        functools.partial(_mix_ffn_kernel, tiles_per_seq=tiles_per_seq),
        grid=(batch, tiles_per_seq),
        in_specs=[pl.BlockSpec((None, ROW_TILE, d), lambda b, t: (b, t, 0)),
                  _resident((N_META, d)),
                  window(HG_WIDTH), window(FOX_WIDTH), window(d), window(d),
                  meta_rows(HG_WIDTH), meta_rows(FOX_WIDTH), meta_rows(d), meta_rows(d),
                  _resident((HG_WIDTH, d)), _resident((FOX_WIDTH, d)), _resident((d, d)),
                  _resident((1, d)), _resident((d, 2 * D_FF)),
                  _resident((CONV_WIDTH, 2 * D_FF)), _resident((1, 2 * D_FF)),
                  _resident((D_FF, d))],
        out_specs=pl.BlockSpec((None, ROW_TILE, d), lambda b, t: (b, t, 0)),
        out_shape=jax.ShapeDtypeStruct((batch, seq, d), F32),
        scratch_shapes=[pltpu.VMEM((2 * SUBLANES, 2 * D_FF), F32),
                        pltpu.VMEM((d // LANES, ROW_TILE, LANES), F32)],
        compiler_params=pltpu.CompilerParams(
            dimension_semantics=("arbitrary", "arbitrary"),
            vmem_limit_bytes=56 * 1024 * 1024),
        name="mix_ffn",
    )(xf, meta, o_hg, o_fox, ga, gb, o_hg, o_fox, ga, gb,
      w_branch_a[layer].astype(BF16), w_branch_b[layer].astype(BF16),
      w_out[layer].astype(BF16), norm2_gain[layer].astype(F32).reshape(1, d),
      w_up[layer].astype(BF16), conv_w[layer].astype(F32),
      conv_b[layer].astype(F32).reshape(1, 2 * D_FF), w_down[layer].astype(BF16))

    return h2.astype(x.dtype)
```

```python
import functools

import jax
import jax.numpy as jnp
from jax import lax
from jax.experimental import pallas as pl
from jax.experimental.pallas import tpu as pltpu

F32 = jnp.float32
BF16 = jnp.bfloat16

D_MODEL = 1024
N_META = 16
FOX_HEADS = 8
FOX_HEAD_DIM = 64
FOX_WIDTH = FOX_HEADS * FOX_HEAD_DIM
HG_HEADS = 4
HG_DIM = 128
HG_WIDTH = HG_HEADS * HG_DIM
D_FF = 2816
CONV_WIDTH = 3
EPS = 1e-6

LANES = 128
SUBLANES = 8
LEAD = 128
PAD_ROWS = LEAD - N_META
ROW_TILE = 384
GROUP_LANES = 256
HEADS_PER_GROUP = GROUP_LANES // FOX_HEAD_DIM
ATT_BLOCK = 1024
KEY_BLOCK = 256
KEY_UNROLL = 4
LOG2E = 1.4426950408889634
HG_CHUNK = 128
HG_BASE = 16
HG_GROUP = 2
HG_UNROLL = 11
FF_CHUNK = 256
MASK_VALUE = -1e30

C_FQ, C_FK, C_FV, C_HQ, C_HF, C_HI, C_HG, C_GA, C_GB, C_FF, C_END = (
    0, 512, 1024, 1536, 2048, 2560, 3072, 3584, 4608, 5632, 5760)

NT_DIMS = (((1,), (1,)), ((), ()))


def _sigmoid(x):
    return 1.0 / (1.0 + jnp.exp(-x))


def _rms_rows(x, gain):
    ms = jnp.mean(x * x, axis=-1, keepdims=True)
    return x * lax.rsqrt(ms + EPS) * gain


def _padded_tile(x_ref, meta_ref, tile_in_seq):
    xb = x_ref[...]
    tm, d = xb.shape
    lead = jnp.concatenate([jnp.zeros((PAD_ROWS, d), F32), meta_ref[...]], axis=0)
    first = jnp.concatenate([lead, xb[:tm - LEAD, :]], axis=0)
    return jnp.where(tile_in_seq == 0, first, xb)


def _clamp(i, last):
    return i if last is None else jnp.minimum(i, last)


def _frame_window_spec(tiles_per_seq, tm, d, last=None):
    assert tm % LEAD == 0

    def index_map(i):
        i = _clamp(i, last)
        return (i // tiles_per_seq,
                jnp.maximum((i % tiles_per_seq) * (tm // LEAD) - 1, 0) * LEAD, 0)

    return pl.BlockSpec((None, pl.Element(tm), pl.Element(d)), index_map)


def _resident(shape):
    nd = len(shape)
    return pl.BlockSpec(shape, lambda *_: (0,) * nd, pipeline_mode=pl.Buffered(1))


def _proj_kernel(x_ref, meta_ref, g1_ref, w_ref, bf_ref, gq_ref, gk_ref, lb_ref, gm_ref,
                 q_ref, k_ref, v_ref, cum_ref, hq_ref, kin_ref, hi_ref, hgs_ref,
                 glog_ref, ga_ref, gb_ref, carry_ref, *, tiles_per_seq):
    tm = x_ref.shape[0]
    tile_in_seq = pl.program_id(0) % tiles_per_seq

    @pl.when(tile_in_seq == 0)
    def _():
        carry_ref[...] = jnp.zeros_like(carry_ref)

    xn = _rms_rows(_padded_tile(x_ref, meta_ref, tile_in_seq), g1_ref[...]).astype(BF16)

    def proj(a, b):
        return jnp.dot(xn, w_ref[:, a:b], preferred_element_type=F32)

    gm = gm_ref[...]

    def head_norm(y, gain):
        msq = jnp.dot((y * y).astype(BF16), gm, preferred_element_type=F32)
        return y * lax.rsqrt(msq + EPS) * gain

    q_ref[...] = head_norm(proj(C_FQ, C_FK), gq_ref[...]).astype(BF16)
    k_ref[...] = head_norm(proj(C_FK, C_FV), gk_ref[...]).astype(BF16)
    v_ref[...] = proj(C_FV, C_HQ).astype(BF16)

    z = proj(C_FF, C_END) + bf_ref[...]
    logf = -(jnp.maximum(-z, 0.0) + jnp.log(1.0 + jnp.exp(-jnp.abs(z))))
    rowi = lax.broadcasted_iota(jnp.int32, (tm, LANES), 0)
    cum = logf
    step = 1
    while step < tm:
        cum = cum + jnp.where(rowi >= step, pltpu.roll(cum, step, 0), 0.0)
        step *= 2
    cum = cum + carry_ref[...]
    cum_ref[...] = cum[:, :FOX_HEADS]
    carry_ref[...] = cum[tm - 1:tm, :]

    hq_ref[...] = proj(C_HQ, C_HF).astype(BF16)
    hf = proj(C_HF, C_HI)
    t = jnp.exp(-jnp.abs(hf))
    r = 1.0 / (1.0 + t)
    tr = t * r
    pos = hf >= 0.0
    sig_pos = jnp.where(pos, r, tr)
    sig_neg = jnp.where(pos, tr, r)
    lb = lb_ref[...]
    glog_ref[...] = jnp.log2(lb + (1.0 - lb) * sig_pos)
    kin_ref[...] = ((1.0 - lb) * sig_neg).astype(BF16)
    hi_ref[...] = proj(C_HI, C_HG).astype(BF16)
    hg = proj(C_HG, C_GA)
    hgs_ref[...] = (hg * _sigmoid(hg)).astype(BF16)

    ga_ref[...] = _sigmoid(proj(C_GA, C_GB)).astype(BF16)
    gb_ref[...] = _sigmoid(proj(C_GB, C_FF)).astype(BF16)


def _attn_kernel(q_ref, k_ref, v_ref, crm_ref, crr_ref, o_ref,
                 qm_ref, acc_ref, m_ref, *, n_real):
    hp = HEADS_PER_GROUP
    lane_head = lax.broadcasted_iota(jnp.int32, (1, GROUP_LANES), 1) // FOX_HEAD_DIM

    def setup(qs, tq):
        qb = q_ref[pl.ds(qs, tq), :]
        for j in range(hp):
            qm_ref[j, :tq, :] = jnp.where(lane_head == j, qb, jnp.zeros_like(qb))
            m_ref[j, :tq, :] = jnp.full((tq, LANES), MASK_VALUE, F32)
            acc_ref[j, :tq, :] = jnp.zeros((tq, GROUP_LANES), F32)

    def block(r0, nr, ks, tk, c0, cr, mask):
        kb = k_ref[pl.ds(ks, tk), :]
        vb = v_ref[pl.ds(ks, tk), :]
        bias = (c0 - cr) * LOG2E
        rows = slice(r0, r0 + nr)
        for j in range(hp):
            s = lax.dot_general(qm_ref[j, rows, :], kb, NT_DIMS,
                                preferred_element_type=F32)
            s = s + bias[j:j + 1, :]
            if mask is not None:
                s = jnp.where(mask, s, MASK_VALUE)
            m_old = m_ref[j, rows, :]
            m_new = jnp.maximum(m_old, jnp.max(s, axis=-1, keepdims=True))
            alpha = jnp.exp2(m_old - m_new)
            p = jnp.exp2(s - jnp.concatenate([m_new] * (tk // LANES), axis=1))
            vj = jnp.where(lane_head == j, vb,
                           jnp.where(lane_head == (j + 1) % hp, 1.0, 0.0).astype(BF16))
            acc_ref[j, rows, :] = (
                jnp.concatenate([alpha] * (GROUP_LANES // LANES), axis=1) * acc_ref[j, rows, :]
                + jnp.dot(p.astype(BF16), vj, preferred_element_type=F32))
            m_ref[j, rows, :] = m_new

    def finalize(qs, tq):
        out = jnp.zeros((tq, GROUP_LANES), F32)
        for j in range(hp):
            acc = acc_ref[j, :tq, :]
            c = ((j + 1) % hp) * FOX_HEAD_DIM
            out = jnp.where(lane_head == j, acc * (1.0 / acc[:, c:c + 1]), out)
        o_ref[pl.ds(qs, tq), :] = out.astype(BF16)

    crm = crm_ref[...]
    row_m = lax.broadcasted_iota(jnp.int32, (LEAD, LEAD), 0)
    col_m = lax.broadcasted_iota(jnp.int32, (LEAD, LEAD), 1)
    setup(0, LEAD)
    block(0, LEAD, 0, LEAD, crm[:, PAD_ROWS:PAD_ROWS + 1], crm,
          (col_m >= PAD_ROWS) & (col_m <= row_m))
    finalize(0, LEAD)

    tq, tk = ATT_BLOCK, KEY_BLOCK
    kpq = tq // tk
    lead_mask = lax.broadcasted_iota(jnp.int32, (tq, LEAD), 1) >= PAD_ROWS

    def q_step(i, carry):
        qs = pl.multiple_of(LEAD + i * tq, LANES)
        c0 = crr_ref[i * kpq][:, 0:1]
        setup(qs, tq)
        block(0, tq, 0, LEAD, c0, crm, lead_mask)

        def k_step(jq, c):
            for d in range(KEY_UNROLL):
                jb = jq * KEY_UNROLL + d
                ks = pl.multiple_of(LEAD + jb * tk, LANES)
                block(0, tq, ks, tk, c0, crr_ref[jb], None)
            return c

        lax.fori_loop(0, i * (kpq // KEY_UNROLL), k_step, 0)
        for d in range(kpq):
            causal = (lax.broadcasted_iota(jnp.int32, (tq, tk), 1) + d * tk
                      <= lax.broadcasted_iota(jnp.int32, (tq, tk), 0))
            block(0, tq, pl.multiple_of(qs + d * tk, LANES), tk, c0,
                  crr_ref[i * kpq + d], causal)
        finalize(qs, tq)
        return carry

    lax.fori_loop(0, n_real, q_step, 0)


def _hgrn_kernel(hq_ref, kin_ref, hi_ref, hgs_ref, gl_ref, gain_ref, o_ref,
                 st_ref, *, n_chunk):
    c_len = HG_CHUNK
    st_ref[...] = jnp.zeros_like(st_ref)
    row = lax.broadcasted_iota(jnp.int32, (c_len, c_len), 0)
    col = lax.broadcasted_iota(jnp.int32, (c_len, c_len), 1)
    rowv = lax.broadcasted_iota(jnp.int32, (c_len, HG_DIM), 0)
    gain = gain_ref[...]

    def ref_rows(g, period, offset):
        parts = [jnp.broadcast_to(g[a * period + offset:a * period + offset + 1, :],
                                  (period, HG_DIM))
                 for a in range(c_len // period)]
        return jnp.concatenate(parts, axis=0)

    def chunk(c, hh):
        rows = pl.ds(pl.multiple_of(c * c_len, c_len), c_len)
        cols = slice(hh * HG_DIM, (hh + 1) * HG_DIM)
        g = gl_ref[rows, cols]
        d = 1
        while d < c_len:
            g = g + jnp.where(rowv >= d, pltpu.roll(g, d, 0), 0.0)
            d *= 2
        hq = hq_ref[rows, cols].astype(F32)
        kin = kin_ref[rows, cols].astype(F32)
        v = hi_ref[rows, cols]

        a_mat = jnp.zeros((c_len, c_len), F32)
        bs = c_len // 2
        while bs >= HG_BASE:
            ref = ref_rows(g, 2 * bs, bs - 1)
            ql = (hq * jnp.exp2(jnp.minimum(g - ref, 0.0))).astype(BF16)
            kl = (kin * jnp.exp2(jnp.minimum(ref - g, 0.0))).astype(BF16)
            al = lax.dot_general(ql, kl, NT_DIMS, preferred_element_type=F32)
            same = (row // (2 * bs)) == (col // (2 * bs))
            ml = same & ((row & bs) != 0) & ((col & bs) == 0)
            a_mat = jnp.where(ml, al, a_mat)
            bs //= 2
        ref = ref_rows(g, HG_BASE, HG_BASE // 2 - 1)
        qd = (hq * jnp.exp2(g - ref)).astype(BF16)
        kd = (kin * jnp.exp2(ref - g)).astype(BF16)
        ad = lax.dot_general(qd, kd, NT_DIMS, preferred_element_type=F32)
        md = ((row // HG_BASE) == (col // HG_BASE)) & (col <= row)
        a_mat = jnp.where(md, ad, a_mat)

        st = st_ref[hh]
        o = jnp.dot(a_mat.astype(BF16), v, preferred_element_type=F32)
        o = o + lax.dot_general((hq * jnp.exp2(g)).astype(BF16), st.astype(BF16),
                                NT_DIMS, preferred_element_type=F32)

        g_tot = g[c_len - 1:c_len, :]
        kt = (kin * jnp.exp2(g_tot - g)).astype(BF16)
        v_t = v.astype(F32).T.astype(BF16)
        st_ref[hh] = st * jnp.exp2(g_tot) + jnp.dot(v_t, kt, preferred_element_type=F32)

        y = _rms_rows(o, gain) * hgs_ref[rows, cols].astype(F32)
        o_ref[rows, cols] = y.astype(BF16)

    def body(cu, carry):
        for u in range(HG_UNROLL):
            for hh in range(HG_GROUP):
                chunk(cu * HG_UNROLL + u, hh)
        return carry

    lax.fori_loop(0, n_chunk // HG_UNROLL, body, 0)


def _mix_kernel(x_ref, meta_ref, ohg_ref, ofox_ref, ga_ref, gb_ref, wa_ref, wb_ref, wo_ref,
                o_ref, *, tiles_per_seq):
    h = _padded_tile(x_ref, meta_ref, pl.program_id(0) % tiles_per_seq)
    y_a = jnp.dot(ohg_ref[...], wa_ref[...], preferred_element_type=F32)
    y_b = jnp.dot(ofox_ref[...], wb_ref[...], preferred_element_type=F32)
    merged = ga_ref[...].astype(F32) * y_a + gb_ref[...].astype(F32) * y_b
    o_ref[...] = h + jnp.dot(merged.astype(BF16), wo_ref[...], preferred_element_type=F32)


def _ffn_kernel(h_ref, hlead_ref, g2_ref, wup_ref, cw_ref, cb_ref, wdn_ref, o_ref,
                carry_ref, perm_ref):
    tm = h_ref.shape[0]
    nv = tm // SUBLANES
    lead_cols = 2 * FF_CHUNK

    @pl.when((pl.program_id(0) == 0) & (pl.program_id(1) == 0))
    def _():
        hl = _rms_rows(hlead_ref[...], g2_ref[...]).astype(BF16)
        for c in range(2 * D_FF // lead_cols):
            cols = slice(c * lead_cols, (c + 1) * lead_cols)
            ul = jnp.dot(hl, wup_ref[:, cols], preferred_element_type=F32)
            carry_ref[0:SUBLANES, cols] = pltpu.roll(ul, 1, 0)
            carry_ref[SUBLANES:2 * SUBLANES, cols] = ul

    n_slab = h_ref.shape[1] // LANES
    for j in range(n_slab):
        perm_ref[j] = h_ref[:, j * LANES:(j + 1) * LANES]
    h = jnp.concatenate(
        [jnp.concatenate([perm_ref[j, pl.ds(k, SUBLANES, stride=nv), :] for k in range(nv)],
                         axis=0) for j in range(n_slab)], axis=1)
    hn = _rms_rows(h, g2_ref[...]).astype(BF16)
    sub = lax.broadcasted_iota(jnp.int32, (SUBLANES, FF_CHUNK), 0)

    def up_cols(lo):
        return jnp.dot(hn, wup_ref[:, lo:lo + FF_CHUNK], preferred_element_type=F32)

    def conv_cols(u, lo):
        cols = slice(lo, lo + FF_CHUNK)
        prev = carry_ref[:, cols]
        carry_ref[:, cols] = u[tm - 2 * SUBLANES:tm, :]

        def wrap(prev8, last8):
            return jnp.where(sub == 0, pltpu.roll(prev8, 1, 0), pltpu.roll(last8, 1, 0))

        f1 = wrap(prev[SUBLANES:, :], u[tm - SUBLANES:tm, :])
        f2 = wrap(prev[:SUBLANES, :], u[tm - 2 * SUBLANES:tm - SUBLANES, :])
        s1 = jnp.concatenate([f1, u[:tm - SUBLANES, :]], axis=0)
        s2 = jnp.concatenate([f2, f1, u[:tm - 2 * SUBLANES, :]], axis=0)
        cw = cw_ref[:, cols]
        return cw[0:1, :] * s2 + cw[1:2, :] * s1 + cw[2:3, :] * u + cb_ref[:, cols]

    n_chunks = D_FF // FF_CHUNK
    nxt = (up_cols(0), up_cols(D_FF))
    acc = h
    for c in range(n_chunks):
        u_g, u_v = nxt
        if c + 1 < n_chunks:
            nxt = (up_cols((c + 1) * FF_CHUNK), up_cols(D_FF + (c + 1) * FF_CHUNK))
        u_gate = conv_cols(u_g, c * FF_CHUNK)
        u_val = conv_cols(u_v, D_FF + c * FF_CHUNK)
        act = (u_gate * _sigmoid(u_gate) * u_val).astype(BF16)
        acc = acc + jnp.dot(act, wdn_ref[c * FF_CHUNK:(c + 1) * FF_CHUNK, :],
                            preferred_element_type=F32)
    for j in range(n_slab):
        for k in range(nv):
            perm_ref[j, pl.ds(k, SUBLANES, stride=nv), :] = acc[
                k * SUBLANES:(k + 1) * SUBLANES, j * LANES:(j + 1) * LANES]
        o_ref[:, j * LANES:(j + 1) * LANES] = perm_ref[j]


def _row_spec(width, last=None):
    return pl.BlockSpec((ROW_TILE, width), lambda i: (_clamp(i, last), 0))


def kernel(x, meta_tokens, norm1_gain, w_in, fox_b_f, q_norm_gain, k_norm_gain,
           hg_lb_logits, hg_out_gain, w_branch_a, w_branch_b, w_out, norm2_gain,
           w_up, conv_w, conv_b, w_down):
    batch, seq, d = x.shape
    lp = LEAD + seq
    rows = batch * lp
    assert d == D_MODEL and lp % ROW_TILE == 0 and seq % ATT_BLOCK == 0
    assert lp % (HG_CHUNK * HG_UNROLL) == 0 and D_FF % FF_CHUNK == 0
    n_tiles = rows // ROW_TILE
    layer = 0

    xf = x.astype(F32)
    meta = meta_tokens.astype(F32)
    tiles_per_seq = lp // ROW_TILE

    w = w_in[layer]
    o_fq, o_fk, o_fv, o_ff = 0, 512, 1024, 1536
    o_hq = o_ff + FOX_HEADS
    o_hf, o_hi, o_hg = o_hq + 512, o_hq + 1024, o_hq + 1536
    o_ga = o_hq + 2048
    o_gb = o_ga + D_MODEL
    w1 = jnp.concatenate([
        w[:, o_fq:o_fk], w[:, o_fk:o_fv], w[:, o_fv:o_ff],
        w[:, o_hq:o_hf], w[:, o_hf:o_hi], w[:, o_hi:o_hg], w[:, o_hg:o_ga],
        w[:, o_ga:o_gb], w[:, o_gb:o_gb + D_MODEL],
        w[:, o_ff:o_hq], jnp.zeros((d, LANES - FOX_HEADS), w.dtype)], axis=1).astype(BF16)
    bf = jnp.concatenate([fox_b_f[layer].astype(F32),
                          jnp.zeros((LANES - FOX_HEADS,), F32)]).reshape(1, LANES)
    scale = FOX_HEAD_DIM ** -0.5 * LOG2E
    gq = (jnp.tile(q_norm_gain[layer].astype(F32), FOX_HEADS) * scale).reshape(1, FOX_WIDTH)
    gk = jnp.tile(k_norm_gain[layer].astype(F32), FOX_HEADS).reshape(1, FOX_WIDTH)
    lower_bounds = jnp.cumsum(jax.nn.softmax(hg_lb_logits.astype(F32), axis=0), axis=0)
    lb = lower_bounds[layer].reshape(1, HG_WIDTH)
    head_id = jnp.arange(FOX_WIDTH) // FOX_HEAD_DIM
    gm = jnp.where(head_id[:, None] == head_id[None, :], 1.0 / FOX_HEAD_DIM, 0.0).astype(BF16)

    bf16_w = lambda width: jax.ShapeDtypeStruct((rows, width), BF16)
    outs = pl.pallas_call(
        functools.partial(_proj_kernel, tiles_per_seq=tiles_per_seq),
        grid=(n_tiles,),
        in_specs=[_frame_window_spec(tiles_per_seq, ROW_TILE, d), _resident((N_META, d)),
                  _resident((1, d)), _resident((d, C_END)),
                  _resident((1, LANES)), _resident((1, FOX_WIDTH)), _resident((1, FOX_WIDTH)),
                  _resident((1, HG_WIDTH)), _resident((FOX_WIDTH, FOX_WIDTH))],
        out_specs=[_row_spec(FOX_WIDTH), _row_spec(FOX_WIDTH), _row_spec(FOX_WIDTH),
                   _row_spec(FOX_HEADS),
                   _row_spec(HG_WIDTH), _row_spec(HG_WIDTH), _row_spec(HG_WIDTH),
                   _row_spec(HG_WIDTH), _row_spec(HG_WIDTH),
                   _row_spec(d), _row_spec(d)],
        out_shape=[bf16_w(FOX_WIDTH), bf16_w(FOX_WIDTH), bf16_w(FOX_WIDTH),
                   jax.ShapeDtypeStruct((rows, FOX_HEADS), F32),
                   bf16_w(HG_WIDTH), bf16_w(HG_WIDTH), bf16_w(HG_WIDTH), bf16_w(HG_WIDTH),
                   jax.ShapeDtypeStruct((rows, HG_WIDTH), F32),
                   bf16_w(d), bf16_w(d)],
        scratch_shapes=[pltpu.VMEM((1, LANES), F32)],
        compiler_params=pltpu.CompilerParams(
            dimension_semantics=("arbitrary",), vmem_limit_bytes=52 * 1024 * 1024),
        name="proj",
    )(xf, meta, norm1_gain[layer].astype(F32).reshape(1, d), w1, bf, gq, gk, lb, gm)
    fq, fk, fv, cum, hq, kin, hi, hgs, glog, ga, gb = outs

    n_grp = FOX_HEADS // HEADS_PER_GROUP
    n_real = seq // ATT_BLOCK
    cum_row = jnp.transpose(cum.reshape(batch, lp, n_grp, HEADS_PER_GROUP), (0, 2, 3, 1))
    cum_row_lead = cum_row[..., :LEAD]
    n_kb = seq // KEY_BLOCK
    cum_row_real = jnp.transpose(
        cum_row[..., LEAD:].reshape(batch, n_grp, HEADS_PER_GROUP, n_kb, KEY_BLOCK),
        (0, 1, 3, 2, 4))
    seq_spec = pl.BlockSpec((lp, GROUP_LANES), lambda b, g: (b, g))
    o_fox = pl.pallas_call(
        functools.partial(_attn_kernel, n_real=n_real),
        grid=(batch, n_grp),
        in_specs=[seq_spec, seq_spec, seq_spec,
                  pl.BlockSpec((None, None, HEADS_PER_GROUP, LEAD), lambda b, g: (b, g, 0, 0)),
                  pl.BlockSpec((None, None, n_kb, HEADS_PER_GROUP, KEY_BLOCK),
                               lambda b, g: (b, g, 0, 0, 0))],
        out_specs=seq_spec,
        out_shape=jax.ShapeDtypeStruct((rows, FOX_WIDTH), BF16),
        scratch_shapes=[pltpu.VMEM((HEADS_PER_GROUP, ATT_BLOCK, GROUP_LANES), BF16),
                        pltpu.VMEM((HEADS_PER_GROUP, ATT_BLOCK, GROUP_LANES), F32),
                        pltpu.VMEM((HEADS_PER_GROUP, ATT_BLOCK, LANES), F32)],
        compiler_params=pltpu.CompilerParams(
            dimension_semantics=("parallel", "parallel"), vmem_limit_bytes=48 * 1024 * 1024),
        name="fox_attn",
    )(fq, fk, fv, cum_row_lead, cum_row_real)

    head_spec = pl.BlockSpec((lp, HG_GROUP * HG_DIM), lambda b, hh: (b, hh))
    o_hg = pl.pallas_call(
        functools.partial(_hgrn_kernel, n_chunk=lp // HG_CHUNK),
        grid=(batch, HG_HEADS // HG_GROUP),
        in_specs=[head_spec, head_spec, head_spec, head_spec, head_spec,
                  pl.BlockSpec((1, HG_DIM), lambda b, hh: (0, 0))],
        out_specs=head_spec,
        out_shape=jax.ShapeDtypeStruct((rows, HG_WIDTH), BF16),
        scratch_shapes=[pltpu.VMEM((HG_GROUP, HG_DIM, HG_DIM), F32)],
        compiler_params=pltpu.CompilerParams(
            dimension_semantics=("parallel", "parallel"), vmem_limit_bytes=40 * 1024 * 1024),
        name="hgrn2",
    )(hq, kin, hi, hgs, glog, hg_out_gain[layer].astype(F32).reshape(1, HG_DIM))

    last = n_tiles - 1
    h1 = pl.pallas_call(
        functools.partial(_mix_kernel, tiles_per_seq=tiles_per_seq),
        grid=(n_tiles + 1,),
        in_specs=[_frame_window_spec(tiles_per_seq, ROW_TILE, d, last), _resident((N_META, d)),
                  _row_spec(HG_WIDTH, last), _row_spec(FOX_WIDTH, last),
                  _row_spec(d, last), _row_spec(d, last),
                  _resident((HG_WIDTH, d)), _resident((FOX_WIDTH, d)), _resident((d, d))],
        out_specs=_row_spec(d),
        out_shape=jax.ShapeDtypeStruct((rows + ROW_TILE, d), F32),
        compiler_params=pltpu.CompilerParams(
            dimension_semantics=("parallel",), vmem_limit_bytes=40 * 1024 * 1024),
        name="mix",
    )(xf, meta, o_hg, o_fox, ga, gb, w_branch_a[layer].astype(BF16),
      w_branch_b[layer].astype(BF16), w_out[layer].astype(BF16))

    h2 = pl.pallas_call(
        _ffn_kernel,
        grid=(batch, tiles_per_seq),
        in_specs=[pl.BlockSpec((pl.Element(ROW_TILE), pl.Element(d)),
                               lambda b, t: ((b * (lp // LEAD) + 1 + t * (ROW_TILE // LEAD))
                                             * LEAD, 0)),
                  pl.BlockSpec((pl.Element(SUBLANES), pl.Element(d)),
                               lambda b, t: (LEAD - SUBLANES, 0)),
                  _resident((1, d)), _resident((d, 2 * D_FF)),
                  _resident((CONV_WIDTH, 2 * D_FF)), _resident((1, 2 * D_FF)),
                  _resident((D_FF, d))],
        out_specs=pl.BlockSpec((None, ROW_TILE, d), lambda b, t: (b, t, 0)),
        out_shape=jax.ShapeDtypeStruct((batch, seq, d), F32),
        scratch_shapes=[pltpu.VMEM((2 * SUBLANES, 2 * D_FF), F32),
                        pltpu.VMEM((d // LANES, ROW_TILE, LANES), F32)],
        compiler_params=pltpu.CompilerParams(
            dimension_semantics=("arbitrary", "arbitrary"),
            vmem_limit_bytes=52 * 1024 * 1024),
        name="ffn",
    )(h1, h1, norm2_gain[layer].astype(F32).reshape(1, d), w_up[layer].astype(BF16),
      conv_w[layer].astype(F32), conv_b[layer].astype(F32).reshape(1, 2 * D_FF),
      w_down[layer].astype(BF16))

    return h2.astype(x.dtype)
```
